```python
import jax, jax.numpy as jnp
from jax import lax
import numpy as np

D_MODEL = 1024
BATCH = 4
SEQ = 4096
DEPTH = 4
DEC_BATCH = 128
DEC_SEQ = 4
PAST_LEN = 2048
PAGE_SIZE = 128

N_MIXERS = 3
HEAD_DIM = 64
N_HEADS = D_MODEL // HEAD_DIM
INNER = N_HEADS * HEAD_DIM
NSA_KV_HEADS = 4
NSA_GROUP = N_HEADS // NSA_KV_HEADS
NSA_KV_W = NSA_KV_HEADS * HEAD_DIM
CMP_BLOCK = 64
CMP_HIDDEN = HEAD_DIM
SEL_TOP = 15
WINDOW = 512
MOBA_BLOCK = 256
MOBA_TOP = 3
RET_HEADS = 4
RET_DK = D_MODEL // RET_HEADS
RET_DV = D_MODEL // RET_HEADS
RET_CHUNK = 128
GATHER_QBLOCK = 32
WIN_QBLOCK = 128
RMS_EPS = 1e-6
GN_EPS = 1e-5
NEG_INF = -1e30
ATTN_SCALE = HEAD_DIM ** -0.5
NSA_SPLIT = (INNER, NSA_KV_W, NSA_KV_W, NSA_KV_W, NSA_KV_W, NSA_KV_W, NSA_KV_W, 3 * N_HEADS, INNER)
NSA_IN = sum(NSA_SPLIT)
MOBA_IN = 4 * INNER
RET_SPLIT = (RET_HEADS * RET_DK, RET_HEADS * RET_DK, RET_HEADS * RET_DV, RET_HEADS * RET_DV)
RET_IN = sum(RET_SPLIT)
N_NSA = len(range(0, DEPTH, N_MIXERS))
N_MOBA = len(range(1, DEPTH, N_MIXERS))
N_RET = len(range(2, DEPTH, N_MIXERS))

kernel_name = 'hybrid_nsa_moba_retention_step'

F32 = jnp.float32


def rms_norm(x, w):
    xf = x.astype(F32)
    y = xf * lax.rsqrt(jnp.mean(xf * xf, axis=-1, keepdims=True) + RMS_EPS)
    return (y * w.astype(F32)).astype(x.dtype)


def alibi_slopes(n):
    return jnp.exp2(-8.0 * (jnp.arange(n, dtype=F32) + 1.0) / n)


def split_cols(t, widths):
    return jnp.split(t, np.cumsum(widths)[:-1].tolist(), axis=-1)


def to_heads(t, g, r):
    b, n, _ = t.shape
    return t.reshape(b, n, g, r, HEAD_DIM).transpose(0, 2, 3, 1, 4)


def from_heads(o):
    b, g, r, n, d = o.shape
    return o.transpose(0, 3, 1, 2, 4).reshape(b, n, g * r * d)


def mixer_output(o, z, w_out, dtype):
    return (from_heads(o).astype(F32) * jax.nn.silu(z.astype(F32))).astype(dtype) @ w_out


def select_blocks(score, own_blk, n_top):
    nb = score.shape[-1]
    cand = jnp.arange(nb)[None, :] < own_blk[:, None]
    _, idx = lax.top_k(jnp.where(cand, score.astype(F32), -jnp.inf), n_top)
    return idx, idx < own_blk[:, None]


def gathered_attention(q, k, v, k_pos, valid, q_pos, slopes):
    s = jnp.einsum('bgrsd,bgsld->bgrsl', q, k).astype(F32) * ATTN_SCALE
    dist = (q_pos[:, None] - k_pos).astype(F32)
    mask = valid & (k_pos <= q_pos[:, None])
    s = jnp.where(mask[:, :, None], s - slopes[:, :, None, None] * dist[:, :, None], NEG_INF)
    p = jax.nn.softmax(s, axis=-1)
    return jnp.einsum('bgrsl,bgsld->bgrsd', p.astype(v.dtype), v)


def window_attention(q, k, v, q_pos, k_pos, slopes):
    s = jnp.einsum('bgrsd,blgd->bgrsl', q, k).astype(F32) * ATTN_SCALE
    diff = q_pos[:, None] - k_pos[None, :]
    mask = (k_pos[None, :] >= 0) & (diff >= 0) & (diff < WINDOW)
    s = jnp.where(mask, s - slopes[:, :, None, None] * diff.astype(F32), NEG_INF)
    p = jax.nn.softmax(s, axis=-1)
    return jnp.einsum('bgrsl,blgd->bgrsd', p.astype(v.dtype), v)


def compress_blocks(rows, pe, w1, w2):
    b, n, g, d = rows.shape
    nb = n // CMP_BLOCK
    blk = rows.reshape(b, nb, CMP_BLOCK, g, d) + pe[None, None, :, None, :]
    flat = blk.transpose(0, 3, 1, 2, 4).reshape(b, g, nb, CMP_BLOCK * d)
    return jax.nn.silu(flat @ w1) @ w2


def compressed_attention(q, kc, vc, q_pos, slopes):
    nb = kc.shape[2]
    s = jnp.einsum('bgrsd,bgnd->bgrsn', q, kc).astype(F32) * ATTN_SCALE
    blk_end = jnp.arange(nb) * CMP_BLOCK + (CMP_BLOCK - 1)
    valid = jnp.arange(nb)[None, :] < (q_pos // CMP_BLOCK)[:, None]
    dist = (q_pos[:, None] - blk_end[None, :]).astype(F32)
    s = jnp.where(valid, s - slopes[:, :, None, None] * dist, NEG_INF)
    p = jax.nn.softmax(s, axis=-1) * valid
    o = jnp.einsum('bgrsn,bgnd->bgrsd', p.astype(vc.dtype), vc)
    return o, p.sum(axis=2)


def gathered_blocks_prompt(q, k, v, idx, valid, q_pos, slopes, blk):
    b, g, r, n, d = q.shape
    m = idx.shape[-1]
    qb = GATHER_QBLOCK
    nc = n // qb
    nbk = -(-n // blk)
    pad = ((0, 0), (0, 0), (0, nbk * blk - n), (0, 0))
    kb = jnp.pad(k, pad).reshape(b, g, nbk, blk, d)
    vb = jnp.pad(v, pad).reshape(b, g, nbk, blk, d)
    bi = jnp.arange(b)[:, None, None, None]
    gi = jnp.arange(g)[None, :, None, None]
    offs = jnp.arange(blk)

    def chunk(args):
        qc, ic, mc, pc = args
        kpos = (ic[..., None] * blk + offs).reshape(b, g, qb, m * blk)
        kg = kb[bi, gi, ic].reshape(b, g, qb, m * blk, d)
        vg = vb[bi, gi, ic].reshape(b, g, qb, m * blk, d)
        return gathered_attention(qc, kg, vg, kpos, jnp.repeat(mc, blk, axis=-1), pc, slopes)

    xs = (jnp.moveaxis(q.reshape(b, g, r, nc, qb, d), 3, 0),
          jnp.moveaxis(idx.reshape(b, g, nc, qb, m), 2, 0),
          jnp.moveaxis(valid.reshape(b, g, nc, qb, m), 2, 0),
          q_pos.reshape(nc, qb))
    o = lax.map(chunk, xs)
    return jnp.moveaxis(o, 0, 3).reshape(b, g, r, n, d)


def window_prompt(q, k, v, slopes):
    b, g, r, n, d = q.shape
    qb = WIN_QBLOCK
    nc = n // qb
    span = WINDOW + qb
    kp = jnp.pad(k, ((0, 0), (WINDOW, 0), (0, 0), (0, 0)))
    vp = jnp.pad(v, ((0, 0), (WINDOW, 0), (0, 0), (0, 0)))

    def chunk(args):
        qc, c = args
        start = c * qb
        kc = lax.dynamic_slice_in_dim(kp, start, span, axis=1)
        vc = lax.dynamic_slice_in_dim(vp, start, span, axis=1)
        return window_attention(qc, kc, vc, start + jnp.arange(qb), start - WINDOW + jnp.arange(span), slopes)

    o = lax.map(chunk, (jnp.moveaxis(q.reshape(b, g, r, nc, qb, d), 3, 0), jnp.arange(nc)))
    return jnp.moveaxis(o, 0, 3).reshape(b, g, r, n, d)


def paged_rows(pool, pt_row, tok):
    gi = jnp.arange(pool.shape[2]).reshape((-1,) + (1,) * (tok.ndim - 1))
    return pool[pt_row[tok // PAGE_SIZE], tok % PAGE_SIZE, gi]


def nsa_project(h, w_in):
    b, n, _ = h.shape
    q, ck, cv, sk, sv, wk, wv, gl, z = split_cols(h @ w_in, NSA_SPLIT)
    kv = lambda t: t.reshape(b, n, NSA_KV_HEADS, HEAD_DIM)
    gates = jax.nn.sigmoid(gl.astype(F32)).reshape(b, n, NSA_KV_HEADS, NSA_GROUP, 3).transpose(0, 2, 3, 1, 4)
    return to_heads(q, NSA_KV_HEADS, NSA_GROUP), kv(ck), kv(cv), kv(sk), kv(sv), kv(wk), kv(wv), gates, z


def merge_branches(g, o_cmp, o_sel, o_win):
    return g[..., 0:1] * o_cmp + g[..., 1:2] * o_sel + g[..., 2:3] * o_win


def nsa_prompt(h, w_in, pe_k, pe_v, w1_k, w2_k, w1_v, w2_v, w_out, slopes):
    b, n, _ = h.shape
    q, ck, cv, sk, sv, wk, wv, gates, z = nsa_project(h, w_in)
    pos = jnp.arange(n)
    kc = compress_blocks(ck, pe_k, w1_k, w2_k)
    vc = compress_blocks(cv, pe_v, w1_v, w2_v)
    o_cmp, imp = compressed_attention(q, kc, vc, pos, slopes)
    own = pos // CMP_BLOCK
    top, top_valid = select_blocks(imp, own, min(SEL_TOP, n // CMP_BLOCK - 1))
    idx = jnp.concatenate([top, jnp.broadcast_to(own[:, None], top.shape[:-1] + (1,))], axis=-1)
    valid = jnp.concatenate([top_valid, jnp.ones(top.shape[:-1] + (1,), bool)], axis=-1)
    o_sel = gathered_blocks_prompt(q, sk.transpose(0, 2, 1, 3), sv.transpose(0, 2, 1, 3), idx, valid, pos, slopes, CMP_BLOCK)
    o_win = window_prompt(q, wk, wv, slopes)
    y = mixer_output(merge_branches(gates, o_cmp, o_sel, o_win), z, w_out, h.dtype)
    keep = min(WINDOW, n)
    return y, (ck, cv, sk, sv, wk[:, n - keep:], wv[:, n - keep:])


def nsa_sample(h, page_table, ck_pool, cv_pool, sk_pool, sv_pool, win_k, win_v,
               w_in, pe_k, pe_v, w1_k, w2_k, w1_v, w2_v, w_out, slopes):
    b, n, _ = h.shape
    assert n <= CMP_BLOCK
    past = page_table.shape[1] * PAGE_SIZE
    q, ck, cv, sk, sv, wk, wv, gates, z = nsa_project(h, w_in)
    pos = past + jnp.arange(n)
    n_top = min(SEL_TOP, past // CMP_BLOCK)
    kw = jnp.concatenate([win_k, wk], axis=1)
    vw = jnp.concatenate([win_v, wv], axis=1)
    kw_pos = past - win_k.shape[1] + jnp.arange(kw.shape[1])
    G, d = NSA_KV_HEADS, HEAD_DIM

    def one_seq(args):
        q1, sk1, sv1, kw1, vw1, g1, pt1 = args
        kc = compress_blocks(ck_pool[pt1].reshape(1, past, G, d), pe_k, w1_k, w2_k)
        vc = compress_blocks(cv_pool[pt1].reshape(1, past, G, d), pe_v, w1_v, w2_v)
        o_cmp, imp = compressed_attention(q1[None], kc, vc, pos, slopes)
        top, top_valid = select_blocks(imp[0], pos // CMP_BLOCK, n_top)
        tok = (top[..., None] * CMP_BLOCK + jnp.arange(CMP_BLOCK)).reshape(G, n, n_top * CMP_BLOCK)
        k_own = jnp.broadcast_to(sk1.transpose(1, 0, 2)[:, None], (G, n, n, d))
        v_own = jnp.broadcast_to(sv1.transpose(1, 0, 2)[:, None], (G, n, n, d))
        kg = jnp.concatenate([paged_rows(sk_pool, pt1, tok), k_own], axis=2)
        vg = jnp.concatenate([paged_rows(sv_pool, pt1, tok), v_own], axis=2)
        kpos = jnp.concatenate([tok, jnp.broadcast_to(pos, (G, n, n))], axis=2)
        valid = jnp.concatenate([jnp.repeat(top_valid, CMP_BLOCK, axis=-1), jnp.ones((G, n, n), bool)], axis=2)
        o_sel = gathered_attention(q1[None], kg[None], vg[None], kpos[None], valid[None], pos, slopes)
        o_win = window_attention(q1[None], kw1[None], vw1[None], pos, kw_pos, slopes)
        return merge_branches(g1[None], o_cmp, o_sel, o_win)[0]

    o = lax.map(one_seq, (q, sk, sv, kw, vw, gates, page_table))
    y = mixer_output(o, z, w_out, h.dtype)
    keep = min(WINDOW, kw.shape[1])
    return y, (ck, cv, sk, sv, kw[:, kw.shape[1] - keep:], vw[:, vw.shape[1] - keep:])


def moba_project(h, w_in):
    b, n, _ = h.shape
    q, k, v, z = split_cols(h @ w_in, (INNER, INNER, INNER, INNER))
    return to_heads(q, N_HEADS, 1), k.reshape(b, n, N_HEADS, HEAD_DIM), v.reshape(b, n, N_HEADS, HEAD_DIM), z


def moba_prompt(h, w_in, w_out, slopes):
    b, n, _ = h.shape
    q, k, v, z = moba_project(h, w_in)
    kt, vt = k.transpose(0, 2, 1, 3), v.transpose(0, 2, 1, 3)
    pos = jnp.arange(n)
    own = pos // MOBA_BLOCK
    idx = jnp.broadcast_to(own[:, None], (b, N_HEADS, n, 1))
    valid = jnp.ones((b, N_HEADS, n, 1), bool)
    n_top = min(MOBA_TOP, (n - 1) // MOBA_BLOCK)
    if n_top > 0:
        nf = n // MOBA_BLOCK
        kmean = kt[:, :, :nf * MOBA_BLOCK].reshape(b, N_HEADS, nf, MOBA_BLOCK, HEAD_DIM).astype(F32).mean(axis=3)
        score = jnp.einsum('bhsd,bhnd->bhsn', q[:, :, 0].astype(F32), kmean)
        top, top_valid = select_blocks(score, own, n_top)
        idx = jnp.concatenate([top, idx], axis=-1)
        valid = jnp.concatenate([top_valid, valid], axis=-1)
    o = gathered_blocks_prompt(q, kt, vt, idx, valid, pos, slopes, MOBA_BLOCK)
    return mixer_output(o, z, w_out, h.dtype), (k, v)


def moba_sample(h, page_table, k_pool, v_pool, w_in, w_out, slopes):
    b, n, _ = h.shape
    past = page_table.shape[1] * PAGE_SIZE
    q, k, v, z = moba_project(h, w_in)
    pos = past + jnp.arange(n)
    nf = past // MOBA_BLOCK
    own_start = nf * MOBA_BLOCK
    assert past - own_start + n <= MOBA_BLOCK
    n_top = min(MOBA_TOP, nf)
    H, d = N_HEADS, HEAD_DIM

    def one_seq(args):
        q1, k1, v1, pt1 = args
        k_past = k_pool[pt1].reshape(past, H, d)
        v_part = v_pool[pt1[own_start // PAGE_SIZE:]].reshape(past - own_start, H, d)
        k_own = jnp.concatenate([k_past[own_start:], k1], axis=0).transpose(1, 0, 2)
        v_own = jnp.concatenate([v_part, v1], axis=0).transpose(1, 0, 2)
        L = k_own.shape[1]
        kg = jnp.broadcast_to(k_own[:, None], (H, n, L, d))
        vg = jnp.broadcast_to(v_own[:, None], (H, n, L, d))
        kpos = jnp.broadcast_to(own_start + jnp.arange(L), (H, n, L))
        valid = jnp.ones((H, n, L), bool)
        if n_top > 0:
            kmean = k_past[:own_start].reshape(nf, MOBA_BLOCK, H, d).astype(F32).mean(axis=1)
            score = jnp.einsum('hsd,nhd->hsn', q1[:, 0].astype(F32), kmean)
            top, top_valid = select_blocks(score, pos // MOBA_BLOCK, n_top)
            tok = (top[..., None] * MOBA_BLOCK + jnp.arange(MOBA_BLOCK)).reshape(H, n, n_top * MOBA_BLOCK)
            kg = jnp.concatenate([paged_rows(k_pool, pt1, tok), kg], axis=2)
            vg = jnp.concatenate([paged_rows(v_pool, pt1, tok), vg], axis=2)
            kpos = jnp.concatenate([tok, kpos], axis=2)
            valid = jnp.concatenate([jnp.repeat(top_valid, MOBA_BLOCK, axis=-1), valid], axis=2)
        return gathered_attention(q1[None], kg[None], vg[None], kpos[None], valid[None], pos, slopes)[0]

    o = lax.map(one_seq, (q, k, v, page_table))
    return mixer_output(o, z, w_out, h.dtype), (k, v)


def ret_log_gamma():
    return jnp.log1p(-jnp.exp2(-5.0 - jnp.arange(RET_HEADS, dtype=F32)))


def ret_project(h, w_in):
    b, n, _ = h.shape
    q, k, v, z = split_cols(h @ w_in, RET_SPLIT)
    heads = lambda t, d: t.reshape(b, n, RET_HEADS, d).transpose(0, 2, 1, 3).astype(F32)
    return heads(q, RET_DK), heads(k, RET_DK) * RET_DK ** -0.5, heads(v, RET_DV), z


def retention_chunk(state, q, k, v, log_gamma):
    c = q.shape[2]
    i = jnp.arange(c, dtype=F32)
    lg = log_gamma[:, None, None]
    diff = i[:, None] - i[None, :]
    decay = jnp.where(diff >= 0, jnp.exp(lg * jnp.maximum(diff, 0.0)), 0.0)
    inner = jnp.einsum('bhid,bhjd->bhij', q, k) * decay
    o = jnp.einsum('bhij,bhje->bhie', inner, v) + jnp.einsum('bhid,bhde->bhie', q, state) * jnp.exp(lg * (i[:, None] + 1.0))
    new_state = jnp.exp(lg * c) * state + jnp.einsum('bhjd,bhje->bhde', k * jnp.exp(lg * (c - 1.0 - i[:, None])), v)
    return new_state, o


def ret_output(o, z, gn_w, gn_b, w_out, dtype):
    mu = o.mean(axis=-1, keepdims=True)
    var = jnp.mean(jnp.square(o - mu), axis=-1, keepdims=True)
    on = (o - mu) * lax.rsqrt(var + GN_EPS)
    b, hh, n, d = on.shape
    on = on.transpose(0, 2, 1, 3).reshape(b, n, hh * d) * gn_w.astype(F32) + gn_b.astype(F32)
    return (on * jax.nn.silu(z.astype(F32))).astype(dtype) @ w_out


def ret_prompt(h, w_in, gn_w, gn_b, w_out):
    q, k, v, z = ret_project(h, w_in)
    b, hh, n, _ = q.shape
    nc = n // RET_CHUNK
    lg = ret_log_gamma()
    chunks = lambda t: jnp.moveaxis(t.reshape(b, hh, nc, RET_CHUNK, t.shape[-1]), 2, 0)

    def step(state, xs):
        qc, kc, vc = xs
        return retention_chunk(state, qc, kc, vc, lg)

    state, o = lax.scan(step, jnp.zeros((b, hh, RET_DK, RET_DV), F32), (chunks(q), chunks(k), chunks(v)))
    o = jnp.moveaxis(o, 0, 2).reshape(b, hh, n, RET_DV)
    return ret_output(o, z, gn_w, gn_b, w_out, h.dtype), state


def ret_sample(h, state, w_in, gn_w, gn_b, w_out):
    q, k, v, z = ret_project(h, w_in)
    new_state, o = retention_chunk(state.astype(F32), q, k, v, ret_log_gamma())
    return ret_output(o, z, gn_w, gn_b, w_out, h.dtype), new_state


def setup_inputs(seed: int = 0) -> dict:
    key = jax.random.key(seed)
    ks = jax.random.split(key, 40)
    nrm = lambda i, shape, scale: jax.random.normal(ks[i], shape, F32) * scale
    n_pages = PAST_LEN // PAGE_SIZE
    n_used = DEC_BATCH * n_pages
    n_pool = n_used + max(1, n_used // 4)
    page_table = jax.random.permutation(ks[0], n_pool)[:n_used].reshape(DEC_BATCH, n_pages).astype(jnp.int32)
    win = min(WINDOW, PAST_LEN)
    nsa_pool = (N_NSA, n_pool, PAGE_SIZE, NSA_KV_HEADS, HEAD_DIM)
    nsa_win = (N_NSA, DEC_BATCH, win, NSA_KV_HEADS, HEAD_DIM)
    moba_pool = (N_MOBA, n_pool, PAGE_SIZE, N_HEADS, HEAD_DIM)
    return {
        'x_prompt': nrm(1, (BATCH, SEQ, D_MODEL), 1.0),
        'x_sample': nrm(2, (DEC_BATCH, DEC_SEQ, D_MODEL), 1.0),
        'cache_nsa_cmp_k': nrm(3, nsa_pool, 1.0),
        'cache_nsa_cmp_v': nrm(4, nsa_pool, 1.0),
        'cache_nsa_sel_k': nrm(5, nsa_pool, 1.0),
        'cache_nsa_sel_v': nrm(6, nsa_pool, 1.0),
        'cache_nsa_win_k': nrm(7, nsa_win, 1.0),
        'cache_nsa_win_v': nrm(8, nsa_win, 1.0),
        'cache_moba_k': nrm(9, moba_pool, 1.0),
        'cache_moba_v': nrm(10, moba_pool, 1.0),
        'state_ret': nrm(11, (N_RET, DEC_BATCH, RET_HEADS, RET_DK, RET_DV), 0.3),
        'page_table': page_table,
        'norm_w': 1.0 + nrm(12, (DEPTH, D_MODEL), 0.01),
        'final_norm_w': 1.0 + nrm(13, (D_MODEL,), 0.01),
        'nsa_w_in': nrm(14, (N_NSA, D_MODEL, NSA_IN), D_MODEL ** -0.5),
        'nsa_pe_k': nrm(15, (N_NSA, CMP_BLOCK, HEAD_DIM), 0.02),
        'nsa_pe_v': nrm(16, (N_NSA, CMP_BLOCK, HEAD_DIM), 0.02),
        'nsa_w1_k': nrm(17, (N_NSA, CMP_BLOCK * HEAD_DIM, CMP_HIDDEN), (CMP_BLOCK * HEAD_DIM) ** -0.5),
        'nsa_w2_k': nrm(18, (N_NSA, CMP_HIDDEN, HEAD_DIM), CMP_HIDDEN ** -0.5),
        'nsa_w1_v': nrm(19, (N_NSA, CMP_BLOCK * HEAD_DIM, CMP_HIDDEN), (CMP_BLOCK * HEAD_DIM) ** -0.5),
        'nsa_w2_v': nrm(20, (N_NSA, CMP_HIDDEN, HEAD_DIM), CMP_HIDDEN ** -0.5),
        'nsa_w_out': nrm(21, (N_NSA, INNER, D_MODEL), INNER ** -0.5),
        'moba_w_in': nrm(22, (N_MOBA, D_MODEL, MOBA_IN), D_MODEL ** -0.5),
        'moba_w_out': nrm(23, (N_MOBA, INNER, D_MODEL), INNER ** -0.5),
        'ret_w_in': nrm(24, (N_RET, D_MODEL, RET_IN), D_MODEL ** -0.5),
        'ret_gn_w': 1.0 + nrm(25, (N_RET, RET_HEADS * RET_DV), 0.01),
        'ret_gn_b': nrm(26, (N_RET, RET_HEADS * RET_DV), 0.01),
        'ret_w_out': nrm(27, (N_RET, RET_HEADS * RET_DV, D_MODEL), (RET_HEADS * RET_DV) ** -0.5),
    }


def reference(x_prompt, x_sample, cache_nsa_cmp_k, cache_nsa_cmp_v, cache_nsa_sel_k, cache_nsa_sel_v,
              cache_nsa_win_k, cache_nsa_win_v, cache_moba_k, cache_moba_v, state_ret, page_table,
              norm_w, final_norm_w, nsa_w_in, nsa_pe_k, nsa_pe_v, nsa_w1_k, nsa_w2_k, nsa_w1_v, nsa_w2_v,
              nsa_w_out, moba_w_in, moba_w_out, ret_w_in, ret_gn_w, ret_gn_b, ret_w_out):
    slopes = alibi_slopes(N_HEADS)
    nsa_slopes = slopes.reshape(NSA_KV_HEADS, NSA_GROUP)
    moba_slopes = slopes.reshape(N_HEADS, 1)
    xp, xs = x_prompt, x_sample
    nsa_p, nsa_s, moba_p, moba_s, ret_p, ret_s = [], [], [], [], [], []
    for layer in range(DEPTH):
        j = layer // N_MIXERS
        hp = rms_norm(xp, norm_w[layer])
        hs = rms_norm(xs, norm_w[layer])
        if layer % N_MIXERS == 0:
            w = (nsa_w_in[j], nsa_pe_k[j], nsa_pe_v[j], nsa_w1_k[j], nsa_w2_k[j], nsa_w1_v[j], nsa_w2_v[j], nsa_w_out[j])
            yp, stp = nsa_prompt(hp, *w, nsa_slopes)
            ys, sts = nsa_sample(hs, page_table, cache_nsa_cmp_k[j], cache_nsa_cmp_v[j], cache_nsa_sel_k[j],
                                 cache_nsa_sel_v[j], cache_nsa_win_k[j], cache_nsa_win_v[j], *w, nsa_slopes)
            nsa_p.append(stp)
            nsa_s.append(sts)
        elif layer % N_MIXERS == 1:
            yp, stp = moba_prompt(hp, moba_w_in[j], moba_w_out[j], moba_slopes)
            ys, sts = moba_sample(hs, page_table, cache_moba_k[j], cache_moba_v[j], moba_w_in[j], moba_w_out[j], moba_slopes)
            moba_p.append(stp)
            moba_s.append(sts)
        else:
            yp, stp = ret_prompt(hp, ret_w_in[j], ret_gn_w[j], ret_gn_b[j], ret_w_out[j])
            ys, sts = ret_sample(hs, state_ret[j], ret_w_in[j], ret_gn_w[j], ret_gn_b[j], ret_w_out[j])
            ret_p.append(stp)
            ret_s.append(sts)
        xp = xp + yp
        xs = xs + ys
    y_prompt = rms_norm(xp, final_norm_w)
    y_sample = rms_norm(xs, final_norm_w)
    st = lambda seq, i: jnp.stack([s[i] for s in seq])
    return (y_prompt, y_sample,
            st(nsa_p, 0), st(nsa_p, 1), st(nsa_p, 2), st(nsa_p, 3), st(nsa_p, 4), st(nsa_p, 5),
            st(moba_p, 0), st(moba_p, 1), jnp.stack(ret_p),
            st(nsa_s, 0), st(nsa_s, 1), st(nsa_s, 2), st(nsa_s, 3), st(nsa_s, 4), st(nsa_s, 5),
            st(moba_s, 0), st(moba_s, 1), jnp.stack(ret_s))
```

```python
import functools

import jax
import jax.numpy as jnp
import numpy as np
from jax import lax
from jax.experimental import pallas as pl
from jax.experimental.pallas import tpu as pltpu

F32 = jnp.float32
BF16 = jnp.bfloat16

HEAD_DIM = 64
N_HEADS = 16
NSA_KV_HEADS = 4
NSA_GROUP = 4
CMP_BLOCK = 64
SEL_TOP = 15
WINDOW = 512
MOBA_BLOCK = 256
MOBA_TOP = 3
RET_HEADS = 4
RET_DK = 256
RET_CHUNK = 128
PAGE_SIZE = 128
RMS_EPS = 1e-6
GN_EPS = 1e-5
NEG_INF = -1e30
ATTN_SCALE = HEAD_DIM ** -0.5
N_MIXERS = 3

VMEM_LIMIT = 56 * 1024 * 1024


def _cparams(*sem):
    return pltpu.CompilerParams(dimension_semantics=sem, vmem_limit_bytes=VMEM_LIMIT)


def _nt(a, b):
    return lax.dot_general(a, b, (((1,), (1,)), ((), ())), preferred_element_type=F32)


def _nn(a, b):
    return lax.dot_general(a, b, (((1,), (0,)), ((), ())), preferred_element_type=F32)


def _sigmoid(x):
    return 1.0 / (1.0 + jnp.exp(-x))


def _norm_matmul_kernel(x_ref, nw_ref, w_ref, o_ref, xn_ref):
    @pl.when(pl.program_id(1) == 0)
    def _():
        x = x_ref[...]
        ms = jnp.mean(x * x, axis=-1, keepdims=True)
        xn_ref[...] = ((x * lax.rsqrt(ms + RMS_EPS)) * nw_ref[...]).astype(BF16)

    o_ref[...] = _nn(xn_ref[...], w_ref[...])


def norm_matmul(x, nw, w_bf16, tn=256):
    t, d = x.shape
    n = w_bf16.shape[1]
    tm = min(t, 1024)
    return pl.pallas_call(
        _norm_matmul_kernel,
        grid=(t // tm, n // tn),
        in_specs=[pl.BlockSpec((tm, d), lambda i, j: (i, 0)),
                  pl.BlockSpec((1, d), lambda i, j: (0, 0)),
                  pl.BlockSpec((d, tn), lambda i, j: (0, j))],
        out_specs=pl.BlockSpec((tm, tn), lambda i, j: (i, j)),
        out_shape=jax.ShapeDtypeStruct((t, n), F32),
        scratch_shapes=[pltpu.VMEM((tm, d), BF16)],
        compiler_params=_cparams("parallel", "arbitrary"),
        name="norm_matmul",
    )(x, nw.reshape(1, d), w_bf16)


def _out_proj_kernel(*refs, n_o, nsa_gates, final_norm):
    x_ref, z_ref = refs[0], refs[1]
    o_refs = refs[2:2 + n_o]
    pos = 2 + n_o
    if nsa_gates:
        gl_ref = refs[pos]
        pos += 1
    w_ref = refs[pos]
    pos += 1
    if final_norm:
        fw_ref = refs[pos]
        pos += 1
    y_ref = refs[pos]

    if nsa_gates:
        gates = _sigmoid(gl_ref[...])
        tm = gates.shape[0]
        lane = lax.broadcasted_iota(jnp.int32, (tm, 128), 1)
        cols = []
        for vb in range(N_HEADS // 2):
            acc = None
            for br in range(n_o):
                c0 = (2 * vb) * 3 + br
                c1 = (2 * vb + 1) * 3 + br
                g = jnp.where(lane < HEAD_DIM,
                              jnp.broadcast_to(gates[:, c0:c0 + 1], (tm, 128)),
                              jnp.broadcast_to(gates[:, c1:c1 + 1], (tm, 128)))
                term = g * o_refs[br][:, vb * 128:(vb + 1) * 128]
                acc = term if acc is None else acc + term
            cols.append(acc)
        o = jnp.concatenate(cols, axis=1)
    else:
        o = o_refs[0][...]
    z = z_ref[...]
    gated = (o * (z * _sigmoid(z))).astype(BF16)
    y = x_ref[...] + _nn(gated, w_ref[...])
    if final_norm:
        ms = jnp.mean(y * y, axis=-1, keepdims=True)
        y = (y * lax.rsqrt(ms + RMS_EPS)) * fw_ref[...]
    y_ref[...] = y


def out_proj(x, proj, z_blk, o_list, w_bf16, gl_blk=None, final_w=None, tm=256):
    t, d = x.shape
    tm = min(tm, t)
    n_o = len(o_list)
    in_specs = [pl.BlockSpec((tm, d), lambda i: (i, 0)),
                pl.BlockSpec((tm, d), lambda i: (i, z_blk))]
    args = [x, proj]
    for o in o_list:
        in_specs.append(pl.BlockSpec((tm, d), lambda i: (i, 0)))
        args.append(o)
    if gl_blk is not None:
        in_specs.append(pl.BlockSpec((tm, 256), lambda i: (i, gl_blk)))
        args.append(proj)
    in_specs.append(pl.BlockSpec((d, d), lambda i: (0, 0)))
    args.append(w_bf16)
    if final_w is not None:
        in_specs.append(pl.BlockSpec((1, d), lambda i: (0, 0)))
        args.append(final_w.reshape(1, d))
    kern = functools.partial(_out_proj_kernel, n_o=n_o, nsa_gates=gl_blk is not None,
                             final_norm=final_w is not None)
    return pl.pallas_call(
        kern,
        grid=(t // tm,),
        in_specs=in_specs,
        out_specs=pl.BlockSpec((tm, d), lambda i: (i, 0)),
        out_shape=jax.ShapeDtypeStruct((t, d), F32),
        compiler_params=_cparams("parallel"),
        name="out_proj",
    )(*args)


def _ret_tables(c_true, c_pad):
    lg = jnp.log1p(-jnp.exp2(-5.0 - jnp.arange(RET_HEADS, dtype=F32)))[:, None, None]
    i = jnp.arange(c_pad, dtype=F32)
    live = (i < c_true)
    diff = i[:, None] - i[None, :]
    dmat = jnp.where((diff >= 0) & live[:, None] & live[None, :], jnp.exp(lg * jnp.maximum(diff, 0.0)), 0.0)
    qdec = jnp.exp(lg * (i[:, None] + 1.0)) * jnp.ones((1, 1, 128), F32)
    kdec = jnp.where(live[:, None], jnp.exp(lg * (c_true - 1.0 - i[:, None])), 0.0) * jnp.ones((1, 1, 128), F32)
    sdec = jnp.exp(lg * c_true) * jnp.ones((1, 8, 128), F32)
    return dmat, qdec, kdec, sdec


def _ret_chunk(q_ref, k_ref, v_ref, d_ref, qd_ref, kd_ref, sd_ref, gw_ref, gb_ref, o_ref, st_ref):
    shp = q_ref.shape
    q = q_ref[...].reshape(shp[-2], shp[-1])
    k = k_ref[...].reshape(shp[-2], shp[-1]) * (RET_DK ** -0.5)
    v = v_ref[...].reshape(shp[-2], shp[-1])
    qb, kb, vb = q.astype(BF16), k.astype(BF16), v.astype(BF16)
    state = st_ref[...]
    dmat = d_ref[0]
    qdec = qd_ref[0][:, 0:1]
    kdec = kd_ref[0][:, 0:1]
    sdec = sd_ref[0][0:1, 0:1]
    inner = _nt(qb, kb) * dmat
    o = _nn(inner.astype(BF16), vb) + _nn(qb, state.astype(BF16)) * qdec
    kw = (k * kdec).astype(BF16)
    new_state = sdec * state + _nn(kw.T, vb)

    mu = jnp.mean(o, axis=-1, keepdims=True)
    var = jnp.mean(jnp.square(o - mu), axis=-1, keepdims=True)
    on = (o - mu) * lax.rsqrt(var + GN_EPS)
    o_ref[...] = (on * gw_ref[...] + gb_ref[...]).reshape(o_ref.shape)
    return new_state


def _ret_prompt_kernel(q_ref, k_ref, v_ref, d_ref, qd_ref, kd_ref, sd_ref, gw_ref, gb_ref,
                       o_ref, sn_ref, st_ref, *, n_chunks):
    c = pl.program_id(2)

    @pl.when(c == 0)
    def _():
        st_ref[...] = jnp.zeros_like(st_ref)

    new_state = _ret_chunk(q_ref, k_ref, v_ref, d_ref, qd_ref, kd_ref, sd_ref, gw_ref, gb_ref, o_ref, st_ref)
    st_ref[...] = new_state

    @pl.when(c == n_chunks - 1)
    def _():
        sn_ref[0, 0] = new_state


def _ret_sample_kernel(q_ref, k_ref, v_ref, s0_ref, d_ref, qd_ref, kd_ref, sd_ref, gw_ref, gb_ref,
                       o_ref, sn_ref, st_ref):
    st_ref[...] = s0_ref[0, 0]
    sn_ref[0, 0] = _ret_chunk(q_ref, k_ref, v_ref, d_ref, qd_ref, kd_ref, sd_ref, gw_ref, gb_ref, o_ref, st_ref)


def ret_prompt(proj, batch, gn_w, gn_b):
    t = proj.shape[0]
    n = t // batch
    nc = n // RET_CHUNK
    c = RET_CHUNK
    dmat, qdec, kdec, sdec = _ret_tables(c, c)
    hh = RET_HEADS
    tab = lambda r: pl.BlockSpec((1, r, 128), lambda b, h, j: (h, 0, 0))
    return pl.pallas_call(
        functools.partial(_ret_prompt_kernel, n_chunks=nc),
        grid=(batch, hh, nc),
        in_specs=[pl.BlockSpec((c, 256), lambda b, h, j: (b * nc + j, h)),
                  pl.BlockSpec((c, 256), lambda b, h, j: (b * nc + j, hh + h)),
                  pl.BlockSpec((c, 256), lambda b, h, j: (b * nc + j, 2 * hh + h)),
                  pl.BlockSpec((1, c, c), lambda b, h, j: (h, 0, 0)),
                  tab(c), tab(c), tab(8),
                  pl.BlockSpec((1, 256), lambda b, h, j: (0, h)),
                  pl.BlockSpec((1, 256), lambda b, h, j: (0, h))],
        out_specs=[pl.BlockSpec((c, 256), lambda b, h, j: (b * nc + j, h)),
                   pl.BlockSpec((1, 1, RET_DK, 256), lambda b, h, j: (b, h, 0, 0))],
        out_shape=[jax.ShapeDtypeStruct((t, 1024), F32),
                   jax.ShapeDtypeStruct((batch, hh, RET_DK, 256), F32)],
        scratch_shapes=[pltpu.VMEM((RET_DK, 256), F32)],
        compiler_params=_cparams("parallel", "parallel", "arbitrary"),
        name="ret_prompt",
    )(proj, proj, proj, dmat, qdec, kdec, sdec, gn_w.reshape(1, -1), gn_b.reshape(1, -1))


def ret_sample(proj3, state, gn_w, gn_b, n_tok):
    b, cp = proj3.shape[0], proj3.shape[1]
    dmat, qdec, kdec, sdec = _ret_tables(n_tok, cp)
    hh = RET_HEADS
    tab = lambda r: pl.BlockSpec((1, r, 128), lambda s, h: (h, 0, 0))
    return pl.pallas_call(
        _ret_sample_kernel,
        grid=(b, hh),
        in_specs=[pl.BlockSpec((1, cp, 256), lambda s, h: (s, 0, h)),
                  pl.BlockSpec((1, cp, 256), lambda s, h: (s, 0, hh + h)),
                  pl.BlockSpec((1, cp, 256), lambda s, h: (s, 0, 2 * hh + h)),
                  pl.BlockSpec((1, 1, RET_DK, 256), lambda s, h: (s, h, 0, 0)),
                  pl.BlockSpec((1, cp, cp), lambda s, h: (h, 0, 0)),
                  tab(cp), tab(cp), tab(8),
                  pl.BlockSpec((1, 256), lambda s, h: (0, h)),
                  pl.BlockSpec((1, 256), lambda s, h: (0, h))],
        out_specs=[pl.BlockSpec((1, cp, 256), lambda s, h: (s, 0, h)),
                   pl.BlockSpec((1, 1, RET_DK, 256), lambda s, h: (s, h, 0, 0))],
        out_shape=[jax.ShapeDtypeStruct((b, cp, 1024), F32),
                   jax.ShapeDtypeStruct((b, hh, RET_DK, 256), F32)],
        scratch_shapes=[pltpu.VMEM((RET_DK, 256), F32)],
        compiler_params=_cparams("parallel", "parallel"),
        name="ret_sample",
    )(proj3, proj3, proj3, state, dmat, qdec, kdec, sdec, gn_w.reshape(1, -1), gn_b.reshape(1, -1))


HB = 4


def _flash_kernel(slopes_ref, q_ref, k_ref, v_ref, *rest, tq, tk, kv_shared, use_mask, mask_shared, window):
    if use_mask:
        mask_ref, et_ref, o_ref, m_ref, l_ref, acc_ref = rest
    else:
        o_ref, m_ref, l_ref, acc_ref = rest
    g = pl.program_id(1)
    q0 = pl.program_id(2) * tq
    m_ref[...] = jnp.full(m_ref.shape, NEG_INF, F32)
    l_ref[...] = jnp.zeros(l_ref.shape, F32)
    acc_ref[...] = jnp.zeros(acc_ref.shape, F32)
    qs = [(q_ref[:, r * HEAD_DIM:(r + 1) * HEAD_DIM] * ATTN_SCALE).astype(BF16) for r in range(HB)]
    qpos = q0 + lax.broadcasted_iota(jnp.int32, (tq, 1), 0)
    q_last = q0 + tq - 1
    lo = jnp.maximum(q0 - window + 1, 0) // tk if window else 0
    hi = (q0 + tq + tk - 1) // tk

    def body(j, carry):
        k0 = pl.multiple_of(j * tk, tk)
        kpos = k0 + lax.broadcasted_iota(jnp.int32, (1, tk), 1)
        ok = kpos <= qpos
        if window:
            ok = ok & ((qpos - kpos) < window)
        base = jnp.where(ok, 0.0, NEG_INF)
        krel = (kpos - q_last).astype(F32)
        if use_mask:
            et = et_ref[pl.ds(k0, tk), :]
            if mask_shared:
                base = base + _nt(mask_ref[0, 0], et)
        for r in range(HB):
            hk = 0 if kv_shared else r
            kk = k_ref[0, hk, pl.ds(k0, tk), :]
            vv = v_ref[0, hk, pl.ds(k0, tk), :]
            s = _nt(qs[r], kk) + base + slopes_ref[g * HB + r] * krel
            if use_mask and not mask_shared:
                s = s + _nt(mask_ref[0, r], et)
            m_prev = m_ref[r]
            m_new = jnp.maximum(m_prev, jnp.max(s, axis=-1, keepdims=True))
            alpha = jnp.exp(m_prev - m_new)
            p = jnp.exp(s - m_new)
            l_ref[r] = alpha * l_ref[r] + jnp.sum(p, axis=-1, keepdims=True)
            acc_ref[r] = alpha * acc_ref[r] + _nn(p.astype(BF16), vv)
            m_ref[r] = m_new
        return carry

    lax.fori_loop(lo, hi, body, 0)
    for r in range(HB):
        o_ref[:, r * HEAD_DIM:(r + 1) * HEAD_DIM] = acc_ref[r] / l_ref[r]


def flash_prompt(proj, q_blk0, kh, vh, slopes, batch, mask=None, et=None, window=0, tq=128, tk=256):
    seq = kh.shape[2]
    n_groups = N_HEADS // HB
    nq = seq // tq
    kv_shared = kh.shape[1] == n_groups
    use_mask = mask is not None
    mask_shared = use_mask and mask.shape[1] == n_groups
    kvb = 1 if kv_shared else HB
    in_specs = [pl.BlockSpec((tq, HB * HEAD_DIM), lambda b, g, i, s: (b * nq + i, q_blk0 + g)),
                pl.BlockSpec((1, kvb, seq, HEAD_DIM), lambda b, g, i, s: (b, g, 0, 0)),
                pl.BlockSpec((1, kvb, seq, HEAD_DIM), lambda b, g, i, s: (b, g, 0, 0))]
    args = [proj, kh, vh]
    if use_mask:
        nb = mask.shape[-1]
        mb = 1 if mask_shared else HB
        in_specs += [pl.BlockSpec((1, mb, tq, nb), lambda b, g, i, s: (b, g, i, 0)),
                     pl.BlockSpec((seq, nb), lambda b, g, i, s: (0, 0))]
        args += [mask, et]
    kern = functools.partial(_flash_kernel, tq=tq, tk=tk, kv_shared=kv_shared, use_mask=use_mask,
                             mask_shared=mask_shared, window=window)
    return pl.pallas_call(
        kern,
        grid_spec=pltpu.PrefetchScalarGridSpec(
            num_scalar_prefetch=1,
            grid=(batch, n_groups, nq),
            in_specs=in_specs,
            out_specs=pl.BlockSpec((tq, HB * HEAD_DIM), lambda b, g, i, s: (b * nq + i, g)),
            scratch_shapes=[pltpu.VMEM((HB, tq, 1), F32), pltpu.VMEM((HB, tq, 1), F32),
                            pltpu.VMEM((HB, tq, HEAD_DIM), F32)]),
        out_shape=jax.ShapeDtypeStruct((batch * seq, N_HEADS * HEAD_DIM), F32),
        compiler_params=_cparams("parallel", "parallel", "arbitrary"),
        name="flash_prompt",
    )(slopes, *args)


def _rank_select(v, cand, blk, n_blocks, n_top):
    v = jnp.where(cand, v, -jnp.inf)
    rank = jnp.zeros(v.shape, jnp.int32)
    for i in range(n_blocks):
        vi = v[i:i + 1, :]
        ahead = (vi > v) | ((vi == v) & (blk > i))
        rank = rank + ahead.astype(jnp.int32)
    return cand & (rank < n_top)


def _moba_select_kernel(q_ref, km_ref, o_ref, *, tq, n_blocks):
    q0 = pl.program_id(2) * tq
    own = (q0 + lax.broadcasted_iota(jnp.int32, (1, tq), 1)) // MOBA_BLOCK
    blk = lax.broadcasted_iota(jnp.int32, (n_blocks, 1), 0)
    for r in range(HB):
        q = q_ref[:, r * HEAD_DIM:(r + 1) * HEAD_DIM].astype(BF16)
        score = _nt(km_ref[0, r], q)
        sel = _rank_select(score, blk < own, blk, n_blocks, MOBA_TOP) | (blk == own)
        o_ref[0, r] = jnp.where(sel, 0.0, NEG_INF).astype(BF16)


def moba_select(proj, kmean, batch, tq=256):
    nb = kmean.shape[2]
    seq = proj.shape[0] // batch
    nq = seq // tq
    return pl.pallas_call(
        functools.partial(_moba_select_kernel, tq=tq, n_blocks=nb),
        grid=(batch, N_HEADS // HB, nq),
        in_specs=[pl.BlockSpec((tq, HB * HEAD_DIM), lambda b, g, i: (b * nq + i, g)),
                  pl.BlockSpec((1, HB, nb, HEAD_DIM), lambda b, g, i: (b, g, 0, 0))],
        out_specs=pl.BlockSpec((1, HB, nb, tq), lambda b, g, i: (b, g, 0, i)),
        out_shape=jax.ShapeDtypeStruct((batch, N_HEADS, nb, seq), BF16),
        compiler_params=_cparams("parallel", "parallel", "parallel"),
        name="moba_select",
    )(proj, kmean)


def _block_mean_kernel(k_ref, o_ref):
    o_ref[0] = jnp.mean(k_ref[...], axis=0, keepdims=True)


def block_mean(proj, col_blk, rows):
    t = proj.shape[0]
    return pl.pallas_call(
        _block_mean_kernel,
        grid=(t // rows,),
        in_specs=[pl.BlockSpec((rows, 1024), lambda i: (i, col_blk))],
        out_specs=pl.BlockSpec((1, 1, 1024), lambda i: (i, 0, 0)),
        out_shape=jax.ShapeDtypeStruct((t // rows, 1, 1024), F32),
        compiler_params=_cparams("parallel"),
        name="block_mean",
    )(proj)


def _nsa_cmp_kernel(slopes_ref, q_ref, kc_ref, vc_ref, o_ref, sel_ref, *, tq, n_blocks):
    g = pl.program_id(1)
    q0 = pl.program_id(2) * tq
    kc = kc_ref[0, 0]
    vc = vc_ref[0, 0]
    qpos_l = q0 + lax.broadcasted_iota(jnp.int32, (1, tq), 1)
    blk_s = lax.broadcasted_iota(jnp.int32, (n_blocks, 1), 0)
    own_l = qpos_l // CMP_BLOCK
    valid_t = blk_s < own_l
    dist_t = (qpos_l - (blk_s * CMP_BLOCK + (CMP_BLOCK - 1))).astype(F32)
    qpos_s = q0 + lax.broadcasted_iota(jnp.int32, (tq, 1), 0)
    blk_l = lax.broadcasted_iota(jnp.int32, (1, n_blocks), 1)
    valid = blk_l < (qpos_s // CMP_BLOCK)
    dist = (qpos_s - (blk_l * CMP_BLOCK + (CMP_BLOCK - 1))).astype(F32)
    imp_t = jnp.zeros((n_blocks, tq), F32)
    for r in range(HB):
        slope = slopes_ref[g * HB + r]
        q = (q_ref[:, r * HEAD_DIM:(r + 1) * HEAD_DIM] * ATTN_SCALE).astype(BF16)
        s_t = jnp.where(valid_t, _nt(kc, q) - slope * dist_t, NEG_INF)
        e_t = jnp.where(valid_t, jnp.exp(s_t - jnp.max(s_t, axis=0, keepdims=True)), 0.0)
        imp_t = imp_t + e_t / jnp.maximum(jnp.sum(e_t, axis=0, keepdims=True), 1e-30)
        s = jnp.where(valid, _nt(q, kc) - slope * dist, NEG_INF)
        e = jnp.where(valid, jnp.exp(s - jnp.max(s, axis=-1, keepdims=True)), 0.0)
        p = e / jnp.maximum(jnp.sum(e, axis=-1, keepdims=True), 1e-30)
        o_ref[:, r * HEAD_DIM:(r + 1) * HEAD_DIM] = _nn(p.astype(BF16), vc)
    sel = _rank_select(imp_t, valid_t, blk_s, n_blocks, SEL_TOP) | (blk_s == own_l)
    sel_ref[0, 0] = jnp.where(sel, 0.0, NEG_INF).astype(BF16)


def nsa_cmp_prompt(proj, kc, vc, slopes, batch, tq=128):
    nb = kc.shape[2]
    seq = proj.shape[0] // batch
    nq = seq // tq
    return pl.pallas_call(
        functools.partial(_nsa_cmp_kernel, tq=tq, n_blocks=nb),
        grid_spec=pltpu.PrefetchScalarGridSpec(
            num_scalar_prefetch=1,
            grid=(batch, NSA_KV_HEADS, nq),
            in_specs=[pl.BlockSpec((tq, HB * HEAD_DIM), lambda b, g, i, s: (b * nq + i, g)),
                      pl.BlockSpec((1, 1, nb, HEAD_DIM), lambda b, g, i, s: (b, g, 0, 0)),
                      pl.BlockSpec((1, 1, nb, HEAD_DIM), lambda b, g, i, s: (b, g, 0, 0))],
            out_specs=[pl.BlockSpec((tq, HB * HEAD_DIM), lambda b, g, i, s: (b * nq + i, g)),
                       pl.BlockSpec((1, 1, nb, tq), lambda b, g, i, s: (b, g, 0, i))]),
        out_shape=[jax.ShapeDtypeStruct((batch * seq, N_HEADS * HEAD_DIM), F32),
                   jax.ShapeDtypeStruct((batch, NSA_KV_HEADS, nb, seq), BF16)],
        compiler_params=_cparams("parallel", "parallel", "parallel"),
        name="nsa_cmp_prompt",
    )(slopes, proj, kc, vc)


SEQ_PER_BATCH = 8


def _compress_kernel(pt_ref, *refs, n_pages, kchunk):
    page_refs = refs[:n_pages]
    pe_ref, w1_ref, w2_ref, o_ref, stage_ref = refs[n_pages:]
    s = pl.program_id(0)
    slot = s % SEQ_PER_BATCH
    bpp = page_refs[0].shape[1]
    for p in range(n_pages):
        stage_ref[slot, p * bpp:(p + 1) * bpp, :] = page_refs[p][0]

    @pl.when(slot == SEQ_PER_BATCH - 1)
    def _():
        rows = SEQ_PER_BATCH * n_pages * bpp
        width = stage_ref.shape[-1]
        hid = jnp.zeros((rows, w1_ref.shape[1]), F32)
        for c in range(width // kchunk):
            x = stage_ref[:, :, c * kchunk:(c + 1) * kchunk].reshape(rows, kchunk)
            x = (x + pe_ref[:, c * kchunk:(c + 1) * kchunk]).astype(BF16)
            hid = hid + _nn(x, w1_ref[c * kchunk:(c + 1) * kchunk, :])
        act = (hid * _sigmoid(hid)).astype(BF16)
        o_ref[...] = _nn(act, w2_ref[...])


def compress_paged(pool2, page_table, pe_big, w1_big, w2_big):
    n_seq, n_pages = page_table.shape
    bpp, width = pool2.shape[1], pool2.shape[2]
    rows = SEQ_PER_BATCH * n_pages * bpp
    gw = w1_big.shape[1]
    in_specs = [pl.BlockSpec((1, bpp, width), functools.partial(lambda s, pt, p: (pt[s, p], 0, 0), p=p))
                for p in range(n_pages)]
    in_specs += [pl.BlockSpec((1, width), lambda s, pt: (0, 0)),
                 pl.BlockSpec((width, gw), lambda s, pt: (0, 0)),
                 pl.BlockSpec((gw, gw), lambda s, pt: (0, 0))]
    return pl.pallas_call(
        functools.partial(_compress_kernel, n_pages=n_pages, kchunk=2048),
        grid_spec=pltpu.PrefetchScalarGridSpec(
            num_scalar_prefetch=1,
            grid=(n_seq,),
            in_specs=in_specs,
            out_specs=pl.BlockSpec((rows, gw), lambda s, pt: (s // SEQ_PER_BATCH, 0)),
            scratch_shapes=[pltpu.VMEM((SEQ_PER_BATCH, n_pages * bpp, width), F32)]),
        out_shape=jax.ShapeDtypeStruct((n_seq * n_pages * bpp, gw), F32),
        compiler_params=_cparams("arbitrary"),
        name="compress_paged",
    )(page_table, *([pool2] * n_pages), pe_big, w1_big, w2_big)


COLS = 128


def _tn(a, b):
    return lax.dot_general(a, b, (((0,), (0,)), ((), ())), preferred_element_type=F32)


def _softmax_cols(s):
    e = jnp.exp(s - jnp.max(s, axis=0, keepdims=True))
    return e / jnp.sum(e, axis=0, keepdims=True)


def _nsa_sample_cmp_kernel(qbd_ref, kc_ref, vc_ref, cs_ref, cp_ref, o_ref, mask_ref, *, n_blocks):
    qbd = qbd_ref[0]
    kc = kc_ref[0].astype(BF16)
    vc = vc_ref[0].astype(BF16)
    slope = cs_ref[0:1, :]
    qpos = cp_ref[0:1, :]
    blk = lax.broadcasted_iota(jnp.int32, (n_blocks, 1), 0)
    valid = blk < qpos // CMP_BLOCK
    dist = (qpos - (blk * CMP_BLOCK + (CMP_BLOCK - 1))).astype(F32)
    s = jnp.where(valid, _nn(kc, qbd) - slope * dist, NEG_INF)
    e = jnp.where(valid, jnp.exp(s - jnp.max(s, axis=0, keepdims=True)), 0.0)
    p = e / jnp.maximum(jnp.sum(e, axis=0, keepdims=True), 1e-30)
    o_ref[0] = _tn(p.astype(BF16), vc)
    tot = p + pltpu.roll(p, 4, 1) + pltpu.roll(p, 8, 1) + pltpu.roll(p, 12, 1)
    col = lax.broadcasted_iota(jnp.int32, (1, COLS), 1)
    tot = jnp.where(col % 16 >= 12, tot, 0.0)
    imp = tot + pltpu.roll(tot, COLS - 4, 1) + pltpu.roll(tot, COLS - 8, 1) + pltpu.roll(tot, COLS - 12, 1)
    sel = _rank_select(imp, valid, blk, n_blocks, SEL_TOP)
    mask_ref[0] = jnp.where(sel, 0.0, NEG_INF).astype(BF16)


def nsa_sample_cmp(qbd, kc, vc, col_slope, col_pos):
    n_seq, nb, c = kc.shape
    return pl.pallas_call(
        functools.partial(_nsa_sample_cmp_kernel, n_blocks=nb),
        grid=(n_seq,),
        in_specs=[pl.BlockSpec((1, c, COLS), lambda s: (s, 0, 0)),
                  pl.BlockSpec((1, nb, c), lambda s: (s, 0, 0)),
                  pl.BlockSpec((1, nb, c), lambda s: (s, 0, 0)),
                  pl.BlockSpec((8, COLS), lambda s: (0, 0)),
                  pl.BlockSpec((8, COLS), lambda s: (0, 0))],
        out_specs=[pl.BlockSpec((1, COLS, c), lambda s: (s, 0, 0)),
                   pl.BlockSpec((1, nb, COLS), lambda s: (s, 0, 0))],
        out_shape=[jax.ShapeDtypeStruct((n_seq, COLS, c), F32),
                   jax.ShapeDtypeStruct((n_seq, nb, COLS), BF16)],
        compiler_params=_cparams("parallel"),
        name="nsa_sample_cmp",
    )(qbd, kc, vc, col_slope, col_pos)


def _sample_attn_kernel(pt_ref, qbd_ref, *refs, mode, n_chunks, chunk, kpos0, n_new):
    k_refs = refs[:n_chunks]
    v_refs = refs[n_chunks:2 * n_chunks]
    knew_ref, vnew_ref, cs_ref, cp_ref = refs[2 * n_chunks:2 * n_chunks + 4]
    pos = 2 * n_chunks + 4
    if mode == "sel":
        mask_ref = refs[pos]
        pos += 1
    o_ref, st_ref = refs[pos], refs[pos + 1]
    del pt_ref
    qbd = qbd_ref[0]
    slope = cs_ref[0:1, :]
    qpos = cp_ref[0:1, :]
    row = lax.broadcasted_iota(jnp.int32, (chunk, 1), 0)
    n_past = n_chunks * chunk

    if mode == "moba":
        per_blk = MOBA_BLOCK // chunk
        n_blk = n_chunks // per_blk
        means = []
        for j in range(n_blk):
            tot = jnp.sum(k_refs[j * per_blk][0], axis=0, keepdims=True)
            for i in range(1, per_blk):
                tot = tot + jnp.sum(k_refs[j * per_blk + i][0], axis=0, keepdims=True)
            means.append(tot * (1.0 / MOBA_BLOCK))
        kmean = jnp.concatenate(means, axis=0).astype(BF16)
        blk = lax.broadcasted_iota(jnp.int32, (n_blk, 1), 0)
        sel = _rank_select(_nn(kmean, qbd), blk < qpos // MOBA_BLOCK, blk, n_blk, MOBA_TOP)
        moba_mask = jnp.where(sel, 0.0, NEG_INF)

    for p in range(n_chunks):
        kpos = kpos0 + p * chunk + row
        s = _nn(k_refs[p][0].astype(BF16), qbd) - slope * (qpos - kpos).astype(F32)
        if mode == "sel":
            per_chunk = chunk // CMP_BLOCK
            m0 = mask_ref[0, per_chunk * p:per_chunk * p + 1, :].astype(F32)
            m1 = mask_ref[0, per_chunk * p + 1:per_chunk * p + 2, :].astype(F32)
            s = s + jnp.where(row < CMP_BLOCK, m0, m1)
        elif mode == "moba":
            s = s + moba_mask[p // per_blk:p // per_blk + 1, :]
        else:
            diff = qpos - kpos
            s = jnp.where((kpos >= 0) & (diff >= 0) & (diff < WINDOW), s, NEG_INF)
        st_ref[p * chunk:(p + 1) * chunk, :] = s
    tnew = lax.broadcasted_iota(jnp.int32, (8, 1), 0)
    kpos_new = kpos0 + n_past + tnew
    s_new = _nn(knew_ref[0].astype(BF16), qbd) - slope * (qpos - kpos_new).astype(F32)
    st_ref[n_past:n_past + 8, :] = jnp.where((tnew < n_new) & (kpos_new <= qpos), s_new, NEG_INF)

    p_all = _softmax_cols(st_ref[...]).astype(BF16)
    o = _tn(p_all[n_past:n_past + 8, :], vnew_ref[0].astype(BF16))
    for p in range(n_chunks):
        o = o + _tn(p_all[p * chunk:(p + 1) * chunk, :], v_refs[p][0].astype(BF16))
    o_ref[0] = o


def sample_attn(mode, qbd, k_src, v_src, page_table, k_new, v_new, col_slope, col_pos, kpos0, n_new, mask=None):
    n_seq = qbd.shape[0]
    c = k_src.shape[-1]
    chunk = PAGE_SIZE
    if mode == "win":
        n_chunks = k_src.shape[1] // chunk
        src_spec = [pl.BlockSpec((1, chunk, c), functools.partial(lambda s, pt, p: (s, p, 0), p=p))
                    for p in range(n_chunks)]
    else:
        n_chunks = page_table.shape[1]
        src_spec = [pl.BlockSpec((1, chunk, c), functools.partial(lambda s, pt, p: (pt[s, p], 0, 0), p=p))
                    for p in range(n_chunks)]
    in_specs = [pl.BlockSpec((1, c, COLS), lambda s, pt: (s, 0, 0))] + src_spec + src_spec
    in_specs += [pl.BlockSpec((1, 8, c), lambda s, pt: (s, 0, 0)),
                 pl.BlockSpec((1, 8, c), lambda s, pt: (s, 0, 0)),
                 pl.BlockSpec((8, COLS), lambda s, pt: (0, 0)),
                 pl.BlockSpec((8, COLS), lambda s, pt: (0, 0))]
    args = [qbd] + [k_src] * n_chunks + [v_src] * n_chunks + [k_new, v_new, col_slope, col_pos]
    if mode == "sel":
        nb = mask.shape[1]
        in_specs.append(pl.BlockSpec((1, nb, COLS), lambda s, pt: (s, 0, 0)))
        args.append(mask)
    kern = functools.partial(_sample_attn_kernel, mode=mode, n_chunks=n_chunks, chunk=chunk, kpos0=kpos0,
                             n_new=n_new)
    return pl.pallas_call(
        kern,
        grid_spec=pltpu.PrefetchScalarGridSpec(
            num_scalar_prefetch=1,
            grid=(n_seq,),
            in_specs=in_specs,
            out_specs=pl.BlockSpec((1, COLS, c), lambda s, pt: (s, 0, 0)),
            scratch_shapes=[pltpu.VMEM((n_chunks * chunk + 8, COLS), F32)]),
        out_shape=jax.ShapeDtypeStruct((n_seq, COLS, c), F32),
        compiler_params=_cparams("parallel"),
        name="sample_attn_" + mode,
    )(page_table, *args)


NSA_KV_W = NSA_KV_HEADS * HEAD_DIM
NSA_PROJ_W = 3840
NSA_Z_BLK, NSA_KV_BLK0, NSA_GL_BLK = 1, 8, 14


def _heads_major(t2d, batch, n_heads):
    n = t2d.shape[0] // batch
    return t2d.reshape(batch, n, n_heads, HEAD_DIM).transpose(0, 2, 1, 3).astype(BF16)


def _block_onehot(seq, block):
    return (jnp.arange(seq)[:, None] // block == jnp.arange(seq // block)[None, :]).astype(BF16)


def _column_tables(slopes, past, n_tok):
    col = jnp.arange(COLS)
    live = col < N_HEADS * n_tok
    slope = jnp.where(live, slopes[jnp.minimum(col // n_tok, N_HEADS - 1)], 0.0)
    pos = jnp.where(live, past + col % n_tok, past).astype(jnp.int32)
    return jnp.broadcast_to(slope[None, :], (8, COLS)), jnp.broadcast_to(pos[None, :], (8, COLS))


def _expand_query(q2d, n_seq, n_tok, heads_per_key):
    q4 = (q2d * ATTN_SCALE).reshape(n_seq, n_tok, N_HEADS, HEAD_DIM)
    n_keys = N_HEADS // heads_per_key
    owner = (jnp.arange(N_HEADS)[:, None] // heads_per_key == jnp.arange(n_keys)[None, :])
    qbd = jnp.where(owner[None, None, :, None, :], q4[..., None], 0.0)
    qbd = qbd.transpose(0, 4, 3, 2, 1).reshape(n_seq, n_keys * HEAD_DIM, N_HEADS * n_tok)
    return jnp.pad(qbd, ((0, 0), (0, 0), (0, COLS - N_HEADS * n_tok))).astype(BF16)


def _take_own(o_t, n_seq, n_tok, heads_per_key):
    n_keys = N_HEADS // heads_per_key
    o5 = o_t[:, :N_HEADS * n_tok].reshape(n_seq, N_HEADS, n_tok, n_keys, HEAD_DIM)
    hh = jnp.arange(N_HEADS)
    own = o5[:, hh, :, hh // heads_per_key, :]
    return own.transpose(1, 2, 0, 3).reshape(n_seq * n_tok, N_HEADS * HEAD_DIM)


def _pad_rows(t2d, n_seq, n_tok):
    return jnp.pad(t2d.reshape(n_seq, n_tok, -1), ((0, 0), (0, 8 - n_tok), (0, 0)))


def _nsa_weights(w_in, pe_k, pe_v, w1_k, w2_k, w1_v, w2_v):
    d = w_in.shape[0]
    q, kv, gl, z = (w_in[:, :1024], w_in[:, 1024:1024 + 6 * NSA_KV_W],
                    w_in[:, 1024 + 6 * NSA_KV_W:1024 + 6 * NSA_KV_W + 3 * N_HEADS], w_in[:, -1024:])
    pad = jnp.zeros((d, NSA_PROJ_W - (2048 + 6 * NSA_KV_W + 3 * N_HEADS)), w_in.dtype)
    w_re = jnp.concatenate([q, z, kv, gl, pad], axis=1).astype(BF16)
    eye = jnp.eye(NSA_KV_HEADS, dtype=w1_k.dtype)

    def big(pe, w1, w2):
        hid = w1.shape[1]
        w1r = w1.reshape(CMP_BLOCK, HEAD_DIM, hid)
        w1b = (w1r[:, None, :, None, :] * eye[None, :, None, :, None]).reshape(
            CMP_BLOCK * NSA_KV_W, NSA_KV_HEADS * hid).astype(BF16)
        w2b = (w2[None, :, None, :] * eye[:, None, :, None]).reshape(
            NSA_KV_HEADS * hid, NSA_KV_W).astype(BF16)
        peb = jnp.broadcast_to(pe[:, None, :], (CMP_BLOCK, NSA_KV_HEADS, HEAD_DIM)).reshape(1, -1)
        return peb, w1b, w2b

    return w_re, big(pe_k, w1_k, w2_k), big(pe_v, w1_v, w2_v)


def _nsa_prompt(x, nw, w_re, cmp_k, cmp_v, w_out, slopes, batch, final_w):
    t = x.shape[0]
    n = t // batch
    proj = norm_matmul(x, nw, w_re)
    kv = [proj[:, 2048 + NSA_KV_W * i:2048 + NSA_KV_W * (i + 1)] for i in range(6)]
    ck, cv, sk, sv, wk, wv = kv
    row_w = CMP_BLOCK * NSA_KV_W
    bpp = PAGE_SIZE // CMP_BLOCK
    n_pages = t // PAGE_SIZE
    pt = jnp.arange(n_pages, dtype=jnp.int32).reshape(SEQ_PER_BATCH, n_pages // SEQ_PER_BATCH)
    nb = n // CMP_BLOCK
    comp = lambda rows, wts: compress_paged(rows.reshape(n_pages, bpp, row_w), pt, *wts).reshape(
        batch, nb, NSA_KV_HEADS, HEAD_DIM).transpose(0, 2, 1, 3).astype(BF16)
    o_cmp, sel_t = nsa_cmp_prompt(proj, comp(ck, cmp_k), comp(cv, cmp_v), slopes, batch)
    o_sel = flash_prompt(proj, 0, _heads_major(sk, batch, NSA_KV_HEADS), _heads_major(sv, batch, NSA_KV_HEADS),
                         slopes, batch, mask=sel_t.transpose(0, 1, 3, 2), et=_block_onehot(n, CMP_BLOCK))
    o_win = flash_prompt(proj, 0, _heads_major(wk, batch, NSA_KV_HEADS), _heads_major(wv, batch, NSA_KV_HEADS),
                         slopes, batch, window=WINDOW)
    y = out_proj(x, proj, NSA_Z_BLK, [o_cmp, o_sel, o_win], w_out, gl_blk=NSA_GL_BLK, final_w=final_w)
    st = lambda a: a.reshape(batch, n, NSA_KV_HEADS, HEAD_DIM)
    keep = min(WINDOW, n)
    return y, (st(ck), st(cv), st(sk), st(sv), st(wk)[:, n - keep:], st(wv)[:, n - keep:])


def _nsa_sample(x, nw, w_re, cmp_k, cmp_v, w_out, slopes, n_seq, final_w, page_table, pools, win_k, win_v):
    ck_pool, cv_pool, sk_pool, sv_pool = pools
    n_tok = x.shape[0] // n_seq
    past = page_table.shape[1] * PAGE_SIZE
    proj = norm_matmul(x, nw, w_re)
    kv = [proj[:, 2048 + NSA_KV_W * i:2048 + NSA_KV_W * (i + 1)] for i in range(6)]
    ck, cv, sk, sv, wk, wv = kv
    n_pool = ck_pool.shape[0]
    bpp = PAGE_SIZE // CMP_BLOCK
    nb = past // CMP_BLOCK
    comp = lambda pool, wts: compress_paged(pool.reshape(n_pool, bpp, CMP_BLOCK * NSA_KV_W), page_table,
                                            *wts).reshape(n_seq, nb, NSA_KV_W)
    qbd = _expand_query(proj[:, :1024], n_seq, n_tok, NSA_GROUP)
    col_slope, col_pos = _column_tables(slopes, past, n_tok)
    o_cmp, mask = nsa_sample_cmp(qbd, comp(ck_pool, cmp_k), comp(cv_pool, cmp_v), col_slope, col_pos)
    flat = lambda pool: pool.reshape(pool.shape[0], pool.shape[1], NSA_KV_W)
    o_sel = sample_attn("sel", qbd, flat(sk_pool), flat(sv_pool), page_table, _pad_rows(sk, n_seq, n_tok),
                        _pad_rows(sv, n_seq, n_tok), col_slope, col_pos, 0, n_tok, mask=mask)
    o_win = sample_attn("win", qbd, flat(win_k), flat(win_v), page_table, _pad_rows(wk, n_seq, n_tok),
                        _pad_rows(wv, n_seq, n_tok), col_slope, col_pos, past - win_k.shape[1], n_tok)
    own = lambda o: _take_own(o, n_seq, n_tok, NSA_GROUP)
    y = out_proj(x, proj, NSA_Z_BLK, [own(o_cmp), own(o_sel), own(o_win)], w_out, gl_blk=NSA_GL_BLK,
                 final_w=final_w)
    st = lambda a: a.reshape(n_seq, n_tok, NSA_KV_HEADS, HEAD_DIM)
    kw = jnp.concatenate([win_k, st(wk)], axis=1)
    vw = jnp.concatenate([win_v, st(wv)], axis=1)
    keep = min(WINDOW, kw.shape[1])
    return y, (st(ck), st(cv), st(sk), st(sv), kw[:, kw.shape[1] - keep:], vw[:, vw.shape[1] - keep:])


def _moba_prompt(x, nw, w_in, w_out, slopes, batch, final_w):
    t = x.shape[0]
    n = t // batch
    proj = norm_matmul(x, nw, w_in)
    k, v = proj[:, 1024:2048], proj[:, 2048:3072]
    nf = n // MOBA_BLOCK
    kmean = block_mean(proj, 1, MOBA_BLOCK).reshape(batch, nf, N_HEADS, HEAD_DIM).transpose(0, 2, 1, 3).astype(BF16)
    sel_t = moba_select(proj, kmean, batch)
    o = flash_prompt(proj, 0, _heads_major(k, batch, N_HEADS), _heads_major(v, batch, N_HEADS), slopes, batch,
                     mask=sel_t.transpose(0, 1, 3, 2), et=_block_onehot(n, MOBA_BLOCK))
    y = out_proj(x, proj, 3, [o], w_out, final_w=final_w)
    st = lambda a: a.reshape(batch, n, N_HEADS, HEAD_DIM)
    return y, (st(k), st(v))


def _moba_sample(x, nw, w_in, w_out, slopes, n_seq, final_w, page_table, k_pool, v_pool):
    n_tok = x.shape[0] // n_seq
    past = page_table.shape[1] * PAGE_SIZE
    assert past % MOBA_BLOCK == 0 and n_tok <= MOBA_BLOCK
    proj = norm_matmul(x, nw, w_in)
    k, v = proj[:, 1024:2048], proj[:, 2048:3072]
    qbd = _expand_query(proj[:, :1024], n_seq, n_tok, 1)
    col_slope, col_pos = _column_tables(slopes, past, n_tok)
    flat = lambda pool: pool.reshape(pool.shape[0], pool.shape[1], N_HEADS * HEAD_DIM)
    o = sample_attn("moba", qbd, flat(k_pool), flat(v_pool), page_table, _pad_rows(k, n_seq, n_tok),
                    _pad_rows(v, n_seq, n_tok), col_slope, col_pos, 0, n_tok)
    y = out_proj(x, proj, 3, [_take_own(o, n_seq, n_tok, 1)], w_out, final_w=final_w)
    st = lambda a: a.reshape(n_seq, n_tok, N_HEADS, HEAD_DIM)
    return y, (st(k), st(v))


def _ret_prompt_layer(x, nw, w_in, gn_w, gn_b, w_out, batch, final_w):
    proj = norm_matmul(x, nw, w_in)
    on, state = ret_prompt(proj, batch, gn_w, gn_b)
    return out_proj(x, proj, 3, [on], w_out, final_w=final_w), state


def _ret_sample_layer(x, nw, w_in, gn_w, gn_b, w_out, n_seq, final_w, state):
    n_tok = x.shape[0] // n_seq
    proj = norm_matmul(x, nw, w_in)
    on, new_state = ret_sample(_pad_rows(proj, n_seq, n_tok), state, gn_w, gn_b, n_tok)
    on = on[:, :n_tok].reshape(n_seq * n_tok, -1)
    return out_proj(x, proj, 3, [on], w_out, final_w=final_w), new_state


def kernel(x_prompt, x_sample, cache_nsa_cmp_k, cache_nsa_cmp_v, cache_nsa_sel_k, cache_nsa_sel_v,
           cache_nsa_win_k, cache_nsa_win_v, cache_moba_k, cache_moba_v, state_ret, page_table,
           norm_w, final_norm_w, nsa_w_in, nsa_pe_k, nsa_pe_v, nsa_w1_k, nsa_w2_k, nsa_w1_v, nsa_w2_v,
           nsa_w_out, moba_w_in, moba_w_out, ret_w_in, ret_gn_w, ret_gn_b, ret_w_out):
    batch, seq, d = x_prompt.shape
    n_seq, n_tok, _ = x_sample.shape
    depth = norm_w.shape[0]
    slopes = jnp.exp2(-8.0 * (jnp.arange(N_HEADS, dtype=F32) + 1.0) / N_HEADS)
    xp = x_prompt.reshape(batch * seq, d)
    xs = x_sample.reshape(n_seq * n_tok, d)
    nsa_p, nsa_s, moba_p, moba_s, ret_p, ret_s = [], [], [], [], [], []
    for layer in range(depth):
        j = layer // N_MIXERS
        fw = final_norm_w if layer == depth - 1 else None
        nw = norm_w[layer]
        if layer % N_MIXERS == 0:
            w_re, cmp_k, cmp_v = _nsa_weights(nsa_w_in[j], nsa_pe_k[j], nsa_pe_v[j], nsa_w1_k[j], nsa_w2_k[j],
                                              nsa_w1_v[j], nsa_w2_v[j])
            w_out = nsa_w_out[j].astype(BF16)
            xp, stp = _nsa_prompt(xp, nw, w_re, cmp_k, cmp_v, w_out, slopes, batch, fw)
            pools = (cache_nsa_cmp_k[j], cache_nsa_cmp_v[j], cache_nsa_sel_k[j], cache_nsa_sel_v[j])
            xs, sts = _nsa_sample(xs, nw, w_re, cmp_k, cmp_v, w_out, slopes, n_seq, fw, page_table, pools,
                                  cache_nsa_win_k[j], cache_nsa_win_v[j])
            nsa_p.append(stp)
            nsa_s.append(sts)
        elif layer % N_MIXERS == 1:
            w_in, w_out = moba_w_in[j].astype(BF16), moba_w_out[j].astype(BF16)
            xp, stp = _moba_prompt(xp, nw, w_in, w_out, slopes, batch, fw)
            xs, sts = _moba_sample(xs, nw, w_in, w_out, slopes, n_seq, fw, page_table, cache_moba_k[j],
                                   cache_moba_v[j])
            moba_p.append(stp)
            moba_s.append(sts)
        else:
            w_in, w_out = ret_w_in[j].astype(BF16), ret_w_out[j].astype(BF16)
            xp, stp = _ret_prompt_layer(xp, nw, w_in, ret_gn_w[j], ret_gn_b[j], w_out, batch, fw)
            xs, sts = _ret_sample_layer(xs, nw, w_in, ret_gn_w[j], ret_gn_b[j], w_out, n_seq, fw, state_ret[j])
            ret_p.append(stp)
            ret_s.append(sts)
    st = lambda items, i: jnp.stack([s[i] for s in items])
    return (xp.reshape(batch, seq, d), xs.reshape(n_seq, n_tok, d),
            st(nsa_p, 0), st(nsa_p, 1), st(nsa_p, 2), st(nsa_p, 3), st(nsa_p, 4), st(nsa_p, 5),
            st(moba_p, 0), st(moba_p, 1), jnp.stack(ret_p),
            st(nsa_s, 0), st(nsa_s, 1), st(nsa_s, 2), st(nsa_s, 3), st(nsa_s, 4), st(nsa_s, 5),
            st(moba_s, 0), st(moba_s, 1), jnp.stack(ret_s))
```

```python
import functools

import jax
import jax.numpy as jnp
import numpy as np
from jax import lax
from jax.experimental import pallas as pl
from jax.experimental.pallas import tpu as pltpu

F32 = jnp.float32
BF16 = jnp.bfloat16

HEAD_DIM = 64
N_HEADS = 16
NSA_KV_HEADS = 4
NSA_GROUP = 4
CMP_BLOCK = 64
SEL_TOP = 15
WINDOW = 512
MOBA_BLOCK = 256
MOBA_TOP = 3
RET_HEADS = 4
RET_DK = 256
RET_CHUNK = 128
PAGE_SIZE = 128
RMS_EPS = 1e-6
GN_EPS = 1e-5
NEG_INF = -1e30
ATTN_SCALE = HEAD_DIM ** -0.5
N_MIXERS = 3

VMEM_LIMIT = 56 * 1024 * 1024


def _cparams(*sem):
    return pltpu.CompilerParams(dimension_semantics=sem, vmem_limit_bytes=VMEM_LIMIT)


def _nt(a, b):
    return lax.dot_general(a, b, (((1,), (1,)), ((), ())), preferred_element_type=F32)


def _nn(a, b):
    return lax.dot_general(a, b, (((1,), (0,)), ((), ())), preferred_element_type=F32)


def _tn(a, b):
    return lax.dot_general(a, b, (((0,), (0,)), ((), ())), preferred_element_type=F32)


def _sigmoid(x):
    return 1.0 / (1.0 + jnp.exp(-x))


def _norm_matmul_kernel(x_ref, nw_ref, w_ref, o_ref, xn_ref):
    @pl.when(pl.program_id(1) == 0)
    def _():
        x = x_ref[...]
        ms = jnp.mean(x * x, axis=-1, keepdims=True)
        xn_ref[...] = ((x * lax.rsqrt(ms + RMS_EPS)) * nw_ref[...]).astype(BF16)

    o_ref[...] = _nn(xn_ref[...], w_ref[...])


def norm_matmul(x, nw, w_bf16, tn=256):
    t, d = x.shape
    n = w_bf16.shape[1]
    tm = min(t, 1024)
    return pl.pallas_call(
        _norm_matmul_kernel,
        grid=(t // tm, n // tn),
        in_specs=[pl.BlockSpec((tm, d), lambda i, j: (i, 0)),
                  pl.BlockSpec((1, d), lambda i, j: (0, 0)),
                  pl.BlockSpec((d, tn), lambda i, j: (0, j))],
        out_specs=pl.BlockSpec((tm, tn), lambda i, j: (i, j)),
        out_shape=jax.ShapeDtypeStruct((t, n), F32),
        scratch_shapes=[pltpu.VMEM((tm, d), BF16)],
        compiler_params=_cparams("parallel", "arbitrary"),
        name="norm_matmul",
    )(x, nw.reshape(1, d), w_bf16)


def _out_proj_kernel(*refs, n_o, nsa_gates, final_norm):
    x_ref, z_ref = refs[0], refs[1]
    o_refs = refs[2:2 + n_o]
    pos = 2 + n_o
    if nsa_gates:
        gl_ref = refs[pos]
        pos += 1
    w_ref = refs[pos]
    pos += 1
    if final_norm:
        fw_ref = refs[pos]
        pos += 1
    y_ref = refs[pos]

    if nsa_gates:
        gates = _sigmoid(gl_ref[...])
        tm = gates.shape[0]
        lane = lax.broadcasted_iota(jnp.int32, (tm, 128), 1)
        cols = []
        for vb in range(N_HEADS // 2):
            acc = None
            for br in range(n_o):
                c0 = (2 * vb) * 3 + br
                c1 = (2 * vb + 1) * 3 + br
                g = jnp.where(lane < HEAD_DIM,
                              jnp.broadcast_to(gates[:, c0:c0 + 1], (tm, 128)),
                              jnp.broadcast_to(gates[:, c1:c1 + 1], (tm, 128)))
                term = g * o_refs[br][:, vb * 128:(vb + 1) * 128]
                acc = term if acc is None else acc + term
            cols.append(acc)
        o = jnp.concatenate(cols, axis=1)
    else:
        o = o_refs[0][...]
    z = z_ref[...]
    gated = (o * (z * _sigmoid(z))).astype(BF16)
    y = x_ref[...] + _nn(gated, w_ref[...])
    if final_norm:
        ms = jnp.mean(y * y, axis=-1, keepdims=True)
        y = (y * lax.rsqrt(ms + RMS_EPS)) * fw_ref[...]
    y_ref[...] = y


def out_proj(x, proj, z_blk, o_list, w_bf16, gl_blk=None, final_w=None, tm=256):
    t, d = x.shape
    tm = min(tm, t)
    n_o = len(o_list)
    in_specs = [pl.BlockSpec((tm, d), lambda i: (i, 0)),
                pl.BlockSpec((tm, d), lambda i: (i, z_blk))]
    args = [x, proj]
    for o in o_list:
        in_specs.append(pl.BlockSpec((tm, d), lambda i: (i, 0)))
        args.append(o)
    if gl_blk is not None:
        in_specs.append(pl.BlockSpec((tm, 256), lambda i: (i, gl_blk)))
        args.append(proj)
    in_specs.append(pl.BlockSpec((d, d), lambda i: (0, 0)))
    args.append(w_bf16)
    if final_w is not None:
        in_specs.append(pl.BlockSpec((1, d), lambda i: (0, 0)))
        args.append(final_w.reshape(1, d))
    kern = functools.partial(_out_proj_kernel, n_o=n_o, nsa_gates=gl_blk is not None,
                             final_norm=final_w is not None)
    return pl.pallas_call(
        kern,
        grid=(t // tm,),
        in_specs=in_specs,
        out_specs=pl.BlockSpec((tm, d), lambda i: (i, 0)),
        out_shape=jax.ShapeDtypeStruct((t, d), F32),
        compiler_params=_cparams("parallel"),
        name="out_proj",
    )(*args)


def _ret_tables(c_true, c_pad):
    lg = jnp.log1p(-jnp.exp2(-5.0 - jnp.arange(RET_HEADS, dtype=F32)))[:, None, None]
    i = jnp.arange(c_pad, dtype=F32)
    live = (i < c_true)
    diff = i[:, None] - i[None, :]
    dmat = jnp.where((diff >= 0) & live[:, None] & live[None, :], jnp.exp(lg * jnp.maximum(diff, 0.0)), 0.0)
    qdec = jnp.exp(lg * (i[:, None] + 1.0)) * jnp.ones((1, 1, 128), F32)
    kdec = jnp.where(live[:, None], jnp.exp(lg * (c_true - 1.0 - i[:, None])), 0.0) * jnp.ones((1, 1, 128), F32)
    sdec = jnp.exp(lg * c_true) * jnp.ones((1, 8, 128), F32)
    return dmat, qdec, kdec, sdec


def _ret_chunk(q_ref, k_ref, v_ref, d_ref, qd_ref, kd_ref, sd_ref, gw_ref, gb_ref, o_ref, st_ref):
    shp = q_ref.shape
    q = q_ref[...].reshape(shp[-2], shp[-1])
    k = k_ref[...].reshape(shp[-2], shp[-1]) * (RET_DK ** -0.5)
    v = v_ref[...].reshape(shp[-2], shp[-1])
    qb, kb, vb = q.astype(BF16), k.astype(BF16), v.astype(BF16)
    state = st_ref[...]
    dmat = d_ref[0]
    qdec = qd_ref[0][:, 0:1]
    kdec = kd_ref[0][:, 0:1]
    sdec = sd_ref[0][0:1, 0:1]
    inner = _nt(qb, kb) * dmat
    o = _nn(inner.astype(BF16), vb) + _nn(qb, state.astype(BF16)) * qdec
    kw = (k * kdec).astype(BF16)
    new_state = sdec * state + _nn(kw.T, vb)

    mu = jnp.mean(o, axis=-1, keepdims=True)
    var = jnp.mean(jnp.square(o - mu), axis=-1, keepdims=True)
    on = (o - mu) * lax.rsqrt(var + GN_EPS)
    o_ref[...] = (on * gw_ref[...] + gb_ref[...]).reshape(o_ref.shape)
    return new_state


def _ret_prompt_kernel(q_ref, k_ref, v_ref, d_ref, qd_ref, kd_ref, sd_ref, gw_ref, gb_ref,
                       o_ref, sn_ref, st_ref, *, n_chunks):
    c = pl.program_id(2)

    @pl.when(c == 0)
    def _():
        st_ref[...] = jnp.zeros_like(st_ref)

    new_state = _ret_chunk(q_ref, k_ref, v_ref, d_ref, qd_ref, kd_ref, sd_ref, gw_ref, gb_ref, o_ref, st_ref)
    st_ref[...] = new_state

    @pl.when(c == n_chunks - 1)
    def _():
        sn_ref[0, 0] = new_state


def _ret_sample_kernel(q_ref, k_ref, v_ref, s0_ref, d_ref, qd_ref, kd_ref, sd_ref, gw_ref, gb_ref,
                       o_ref, sn_ref, st_ref):
    st_ref[...] = s0_ref[0, 0]
    sn_ref[0, 0] = _ret_chunk(q_ref, k_ref, v_ref, d_ref, qd_ref, kd_ref, sd_ref, gw_ref, gb_ref, o_ref, st_ref)


def ret_prompt(proj, batch, gn_w, gn_b):
    t = proj.shape[0]
    n = t // batch
    nc = n // RET_CHUNK
    c = RET_CHUNK
    dmat, qdec, kdec, sdec = _ret_tables(c, c)
    hh = RET_HEADS
    tab = lambda r: pl.BlockSpec((1, r, 128), lambda b, h, j: (h, 0, 0))
    return pl.pallas_call(
        functools.partial(_ret_prompt_kernel, n_chunks=nc),
        grid=(batch, hh, nc),
        in_specs=[pl.BlockSpec((c, 256), lambda b, h, j: (b * nc + j, h)),
                  pl.BlockSpec((c, 256), lambda b, h, j: (b * nc + j, hh + h)),
                  pl.BlockSpec((c, 256), lambda b, h, j: (b * nc + j, 2 * hh + h)),
                  pl.BlockSpec((1, c, c), lambda b, h, j: (h, 0, 0)),
                  tab(c), tab(c), tab(8),
                  pl.BlockSpec((1, 256), lambda b, h, j: (0, h)),
                  pl.BlockSpec((1, 256), lambda b, h, j: (0, h))],
        out_specs=[pl.BlockSpec((c, 256), lambda b, h, j: (b * nc + j, h)),
                   pl.BlockSpec((1, 1, RET_DK, 256), lambda b, h, j: (b, h, 0, 0))],
        out_shape=[jax.ShapeDtypeStruct((t, 1024), F32),
                   jax.ShapeDtypeStruct((batch, hh, RET_DK, 256), F32)],
        scratch_shapes=[pltpu.VMEM((RET_DK, 256), F32)],
        compiler_params=_cparams("parallel", "parallel", "arbitrary"),
        name="ret_prompt",
    )(proj, proj, proj, dmat, qdec, kdec, sdec, gn_w.reshape(1, -1), gn_b.reshape(1, -1))


def ret_sample(proj3, state, gn_w, gn_b, n_tok):
    b, cp = proj3.shape[0], proj3.shape[1]
    dmat, qdec, kdec, sdec = _ret_tables(n_tok, cp)
    hh = RET_HEADS
    tab = lambda r: pl.BlockSpec((1, r, 128), lambda s, h: (h, 0, 0))
    return pl.pallas_call(
        _ret_sample_kernel,
        grid=(b, hh),
        in_specs=[pl.BlockSpec((1, cp, 256), lambda s, h: (s, 0, h)),
                  pl.BlockSpec((1, cp, 256), lambda s, h: (s, 0, hh + h)),
                  pl.BlockSpec((1, cp, 256), lambda s, h: (s, 0, 2 * hh + h)),
                  pl.BlockSpec((1, 1, RET_DK, 256), lambda s, h: (s, h, 0, 0)),
                  pl.BlockSpec((1, cp, cp), lambda s, h: (h, 0, 0)),
                  tab(cp), tab(cp), tab(8),
                  pl.BlockSpec((1, 256), lambda s, h: (0, h)),
                  pl.BlockSpec((1, 256), lambda s, h: (0, h))],
        out_specs=[pl.BlockSpec((1, cp, 256), lambda s, h: (s, 0, h)),
                   pl.BlockSpec((1, 1, RET_DK, 256), lambda s, h: (s, h, 0, 0))],
        out_shape=[jax.ShapeDtypeStruct((b, cp, 1024), F32),
                   jax.ShapeDtypeStruct((b, hh, RET_DK, 256), F32)],
        scratch_shapes=[pltpu.VMEM((RET_DK, 256), F32)],
        compiler_params=_cparams("parallel", "parallel"),
        name="ret_sample",
    )(proj3, proj3, proj3, state, dmat, qdec, kdec, sdec, gn_w.reshape(1, -1), gn_b.reshape(1, -1))


HB = 4


LANES = 128


def _flash_kernel(slopes_ref, q_ref, k_ref, v_ref, *rest, tq, tk, n_stack, use_mask, window):
    if use_mask:
        mask_ref, et_ref, o_ref, m_ref, acc_ref = rest
    else:
        o_ref, m_ref, acc_ref = rest
    g = pl.program_id(1)
    q0 = pl.program_id(2) * tq
    n_loop = HB // n_stack
    rows = n_stack * tq
    qpos = q0 + lax.broadcasted_iota(jnp.int32, (tq, 1), 0)
    q_last = q0 + tq - 1
    lo = jnp.maximum(q0 - window + 1, 0) // tk if window else 0
    hi = (q0 + tq + tk - 1) // tk

    for i in range(n_loop):
        heads = [i * n_stack + r for r in range(n_stack)]
        q_st = jnp.concatenate([q_ref[:, h * HEAD_DIM:(h + 1) * HEAD_DIM] for h in heads], axis=0)
        q_st = (q_st * ATTN_SCALE).astype(BF16)
        m_ref[...] = jnp.full(m_ref.shape, NEG_INF, F32)
        acc_ref[...] = jnp.zeros(acc_ref.shape, F32)

        def body(j, carry):
            k0 = pl.multiple_of(j * tk, tk)
            kpos = k0 + lax.broadcasted_iota(jnp.int32, (1, tk), 1)
            ok = kpos <= qpos
            if window:
                ok = ok & ((qpos - kpos) < window)
            base = jnp.where(ok, 0.0, NEG_INF)
            if use_mask:
                base = base + _nt(mask_ref[0, 0 if n_loop == 1 else i], et_ref[pl.ds(k0, tk), :])
            krel = (kpos - q_last).astype(F32)
            bias = jnp.concatenate([base + slopes_ref[g * HB + h] * krel for h in heads], axis=0)
            s = _nt(q_st, k_ref[0, i, pl.ds(k0, tk), :]) + bias
            m_prev = m_ref[...]
            m_new = jnp.maximum(m_prev, jnp.max(s, axis=-1, keepdims=True))
            p = jnp.exp(s - jnp.concatenate([m_new] * (tk // LANES), axis=1))
            acc_ref[...] = jnp.exp(m_prev - m_new) * acc_ref[...] + _nn(p.astype(BF16), v_ref[0, i, pl.ds(k0, tk), :])
            m_ref[...] = m_new
            return carry

        lax.fori_loop(lo, hi, body, 0)
        acc = acc_ref[...]
        o = acc[:, :HEAD_DIM] / acc[:, HEAD_DIM:HEAD_DIM + 1]
        for r, h in enumerate(heads):
            o_ref[:, h * HEAD_DIM:(h + 1) * HEAD_DIM] = o[r * tq:(r + 1) * tq]


def flash_prompt(proj, q_blk0, kh, vh, slopes, batch, mask=None, et=None, window=0, tq=128, tk=512):
    seq = kh.shape[2]
    n_groups = N_HEADS // HB
    nq = seq // tq
    n_stack = HB if kh.shape[1] == n_groups else 1
    kvb = HB // n_stack
    rows = n_stack * tq
    use_mask = mask is not None
    in_specs = [pl.BlockSpec((tq, HB * HEAD_DIM), lambda b, g, i, s: (b * nq + i, q_blk0 + g)),
                pl.BlockSpec((1, kvb, seq, HEAD_DIM), lambda b, g, i, s: (b, g, 0, 0)),
                pl.BlockSpec((1, kvb, seq, LANES), lambda b, g, i, s: (b, g, 0, 0))]
    args = [proj, kh, vh]
    if use_mask:
        nb = mask.shape[-1]
        in_specs += [pl.BlockSpec((1, kvb, tq, nb), lambda b, g, i, s: (b, g, i, 0)),
                     pl.BlockSpec((seq, nb), lambda b, g, i, s: (0, 0))]
        args += [mask, et]
    kern = functools.partial(_flash_kernel, tq=tq, tk=tk, n_stack=n_stack, use_mask=use_mask, window=window)
    return pl.pallas_call(
        kern,
        grid_spec=pltpu.PrefetchScalarGridSpec(
            num_scalar_prefetch=1,
            grid=(batch, n_groups, nq),
            in_specs=in_specs,
            out_specs=pl.BlockSpec((tq, HB * HEAD_DIM), lambda b, g, i, s: (b * nq + i, g)),
            scratch_shapes=[pltpu.VMEM((rows, LANES), F32), pltpu.VMEM((rows, LANES), F32)]),
        out_shape=jax.ShapeDtypeStruct((batch * seq, N_HEADS * HEAD_DIM), F32),
        compiler_params=_cparams("parallel", "parallel", "arbitrary"),
        name="flash_prompt",
    )(slopes, *args)


def _rank_select(v, cand, blk, n_blocks, n_top):
    v = jnp.where(cand, v, -jnp.inf)
    rank = jnp.zeros(v.shape, jnp.int32)
    for i in range(n_blocks):
        vi = v[i:i + 1, :]
        ahead = (vi > v) | ((vi == v) & (blk > i))
        rank = rank + ahead.astype(jnp.int32)
    return cand & (rank < n_top)


def _moba_select_kernel(q_ref, km_ref, o_ref, *, tq, n_blocks):
    q0 = pl.program_id(2) * tq
    own = (q0 + lax.broadcasted_iota(jnp.int32, (1, tq), 1)) // MOBA_BLOCK
    blk = lax.broadcasted_iota(jnp.int32, (n_blocks, 1), 0)
    for r in range(HB):
        q = q_ref[:, r * HEAD_DIM:(r + 1) * HEAD_DIM].astype(BF16)
        score = _nt(km_ref[0, r], q)
        sel = _rank_select(score, blk < own, blk, n_blocks, MOBA_TOP) | (blk == own)
        o_ref[0, r] = jnp.where(sel, 0.0, NEG_INF).astype(BF16)


def moba_select(proj, kmean, batch, tq=256):
    nb = kmean.shape[2]
    seq = proj.shape[0] // batch
    nq = seq // tq
    return pl.pallas_call(
        functools.partial(_moba_select_kernel, tq=tq, n_blocks=nb),
        grid=(batch, N_HEADS // HB, nq),
        in_specs=[pl.BlockSpec((tq, HB * HEAD_DIM), lambda b, g, i: (b * nq + i, g)),
                  pl.BlockSpec((1, HB, nb, HEAD_DIM), lambda b, g, i: (b, g, 0, 0))],
        out_specs=pl.BlockSpec((1, HB, nb, tq), lambda b, g, i: (b, g, 0, i)),
        out_shape=jax.ShapeDtypeStruct((batch, N_HEADS, nb, seq), BF16),
        compiler_params=_cparams("parallel", "parallel", "parallel"),
        name="moba_select",
    )(proj, kmean)


def _block_mean_kernel(k_ref, o_ref):
    o_ref[0] = jnp.mean(k_ref[...], axis=0, keepdims=True)


def block_mean(proj, col_blk, rows):
    t = proj.shape[0]
    return pl.pallas_call(
        _block_mean_kernel,
        grid=(t // rows,),
        in_specs=[pl.BlockSpec((rows, 1024), lambda i: (i, col_blk))],
        out_specs=pl.BlockSpec((1, 1, 1024), lambda i: (i, 0, 0)),
        out_shape=jax.ShapeDtypeStruct((t // rows, 1, 1024), F32),
        compiler_params=_cparams("parallel"),
        name="block_mean",
    )(proj)


def _nsa_cmp_kernel(slopes_ref, q_ref, kc_ref, vc_ref, o_ref, sel_ref, *, tq, n_blocks):
    g = pl.program_id(1)
    q0 = pl.program_id(2) * tq
    kc = kc_ref[0, 0]
    vc = vc_ref[0, 0]
    qpos_l = q0 + lax.broadcasted_iota(jnp.int32, (1, tq), 1)
    blk_s = lax.broadcasted_iota(jnp.int32, (n_blocks, 1), 0)
    own_l = qpos_l // CMP_BLOCK
    valid_t = blk_s < own_l
    dist_t = (qpos_l - (blk_s * CMP_BLOCK + (CMP_BLOCK - 1))).astype(F32)
    imp_t = jnp.zeros((n_blocks, tq), F32)
    for r in range(HB):
        slope = slopes_ref[g * HB + r]
        q = (q_ref[:, r * HEAD_DIM:(r + 1) * HEAD_DIM] * ATTN_SCALE).astype(BF16)
        s_t = jnp.where(valid_t, _nt(kc, q) - slope * dist_t, NEG_INF)
        e_t = jnp.where(valid_t, jnp.exp(s_t - jnp.max(s_t, axis=0, keepdims=True)), 0.0)
        p_t = e_t / jnp.maximum(jnp.sum(e_t, axis=0, keepdims=True), 1e-30)
        imp_t = imp_t + p_t
        o_ref[:, r * HEAD_DIM:(r + 1) * HEAD_DIM] = _tn(p_t.astype(BF16), vc)
    sel = _rank_select(imp_t, valid_t, blk_s, n_blocks, SEL_TOP) | (blk_s == own_l)
    sel_ref[0, 0] = jnp.where(sel, 0.0, NEG_INF).astype(BF16)


def nsa_cmp_prompt(proj, kc, vc, slopes, batch, tq=256):
    nb = kc.shape[2]
    seq = proj.shape[0] // batch
    nq = seq // tq
    return pl.pallas_call(
        functools.partial(_nsa_cmp_kernel, tq=tq, n_blocks=nb),
        grid_spec=pltpu.PrefetchScalarGridSpec(
            num_scalar_prefetch=1,
            grid=(batch, NSA_KV_HEADS, nq),
            in_specs=[pl.BlockSpec((tq, HB * HEAD_DIM), lambda b, g, i, s: (b * nq + i, g)),
                      pl.BlockSpec((1, 1, nb, HEAD_DIM), lambda b, g, i, s: (b, g, 0, 0)),
                      pl.BlockSpec((1, 1, nb, HEAD_DIM), lambda b, g, i, s: (b, g, 0, 0))],
            out_specs=[pl.BlockSpec((tq, HB * HEAD_DIM), lambda b, g, i, s: (b * nq + i, g)),
                       pl.BlockSpec((1, 1, nb, tq), lambda b, g, i, s: (b, g, 0, i))]),
        out_shape=[jax.ShapeDtypeStruct((batch * seq, N_HEADS * HEAD_DIM), F32),
                   jax.ShapeDtypeStruct((batch, NSA_KV_HEADS, nb, seq), BF16)],
        compiler_params=_cparams("parallel", "parallel", "parallel"),
        name="nsa_cmp_prompt",
    )(slopes, proj, kc, vc)


SEQ_PER_BATCH = 8


def _compress_kernel(pt_ref, *refs, n_pages, kchunk):
    page_refs = refs[:n_pages]
    pe_ref, w1_ref, w2_ref, o_ref, stage_ref = refs[n_pages:]
    s = pl.program_id(0)
    slot = s % SEQ_PER_BATCH
    bpp = page_refs[0].shape[1]
    for p in range(n_pages):
        stage_ref[slot, p * bpp:(p + 1) * bpp, :] = page_refs[p][0]

    @pl.when(slot == SEQ_PER_BATCH - 1)
    def _():
        rows = SEQ_PER_BATCH * n_pages * bpp
        width = stage_ref.shape[-1]
        hid = jnp.zeros((rows, w1_ref.shape[1]), F32)
        for c in range(width // kchunk):
            x = stage_ref[:, :, c * kchunk:(c + 1) * kchunk].reshape(rows, kchunk)
            x = (x + pe_ref[:, c * kchunk:(c + 1) * kchunk]).astype(BF16)
            hid = hid + _nn(x, w1_ref[c * kchunk:(c + 1) * kchunk, :])
        act = (hid * _sigmoid(hid)).astype(BF16)
        o_ref[...] = _nn(act, w2_ref[...])


def compress_paged(pool2, page_table, pe_big, w1_big, w2_big):
    n_seq, n_pages = page_table.shape
    bpp, width = pool2.shape[1], pool2.shape[2]
    rows = SEQ_PER_BATCH * n_pages * bpp
    gw = w1_big.shape[1]
    in_specs = [pl.BlockSpec((1, bpp, width), functools.partial(lambda s, pt, p: (pt[s, p], 0, 0), p=p))
                for p in range(n_pages)]
    in_specs += [pl.BlockSpec((1, width), lambda s, pt: (0, 0)),
                 pl.BlockSpec((width, gw), lambda s, pt: (0, 0)),
                 pl.BlockSpec((gw, gw), lambda s, pt: (0, 0))]
    return pl.pallas_call(
        functools.partial(_compress_kernel, n_pages=n_pages, kchunk=2048),
        grid_spec=pltpu.PrefetchScalarGridSpec(
            num_scalar_prefetch=1,
            grid=(n_seq,),
            in_specs=in_specs,
            out_specs=pl.BlockSpec((rows, gw), lambda s, pt: (s // SEQ_PER_BATCH, 0)),
            scratch_shapes=[pltpu.VMEM((SEQ_PER_BATCH, n_pages * bpp, width), F32)]),
        out_shape=jax.ShapeDtypeStruct((n_seq * n_pages * bpp, gw), F32),
        compiler_params=_cparams("arbitrary"),
        name="compress_paged",
    )(page_table, *([pool2] * n_pages), pe_big, w1_big, w2_big)


COLS = 128


def _nsa_sample_cmp_kernel(qbd_ref, kc_ref, vc_ref, cs_ref, cp_ref, o_ref, mask_ref, *, n_blocks):
    qbd = qbd_ref[0]
    kc = kc_ref[0].astype(BF16)
    vc = vc_ref[0].astype(BF16)
    slope = cs_ref[0:1, :]
    qpos = cp_ref[0:1, :]
    blk = lax.broadcasted_iota(jnp.int32, (n_blocks, 1), 0)
    valid = blk < qpos // CMP_BLOCK
    dist = (qpos - (blk * CMP_BLOCK + (CMP_BLOCK - 1))).astype(F32)
    s = jnp.where(valid, _nn(kc, qbd) - slope * dist, NEG_INF)
    e = jnp.where(valid, jnp.exp(s - jnp.max(s, axis=0, keepdims=True)), 0.0)
    p = e / jnp.maximum(jnp.sum(e, axis=0, keepdims=True), 1e-30)
    o_ref[0] = _tn(p.astype(BF16), vc)
    tot = p + pltpu.roll(p, 4, 1) + pltpu.roll(p, 8, 1) + pltpu.roll(p, 12, 1)
    col = lax.broadcasted_iota(jnp.int32, (1, COLS), 1)
    tot = jnp.where(col % 16 >= 12, tot, 0.0)
    imp = tot + pltpu.roll(tot, COLS - 4, 1) + pltpu.roll(tot, COLS - 8, 1) + pltpu.roll(tot, COLS - 12, 1)
    sel = _rank_select(imp, valid, blk, n_blocks, SEL_TOP)
    mask_ref[0] = jnp.where(sel, 0.0, NEG_INF).astype(BF16)


def nsa_sample_cmp(qbd, kc, vc, col_slope, col_pos):
    n_seq, nb, c = kc.shape
    return pl.pallas_call(
        functools.partial(_nsa_sample_cmp_kernel, n_blocks=nb),
        grid=(n_seq,),
        in_specs=[pl.BlockSpec((1, c, COLS), lambda s: (s, 0, 0)),
                  pl.BlockSpec((1, nb, c), lambda s: (s, 0, 0)),
                  pl.BlockSpec((1, nb, c), lambda s: (s, 0, 0)),
                  pl.BlockSpec((8, COLS), lambda s: (0, 0)),
                  pl.BlockSpec((8, COLS), lambda s: (0, 0))],
        out_specs=[pl.BlockSpec((1, COLS, c), lambda s: (s, 0, 0)),
                   pl.BlockSpec((1, nb, COLS), lambda s: (s, 0, 0))],
        out_shape=[jax.ShapeDtypeStruct((n_seq, COLS, c), F32),
                   jax.ShapeDtypeStruct((n_seq, nb, COLS), BF16)],
        compiler_params=_cparams("parallel"),
        name="nsa_sample_cmp",
    )(qbd, kc, vc, col_slope, col_pos)


def _rank_select_lanes(v, cand, n_blocks, n_top):
    lane = lax.broadcasted_iota(jnp.int32, (1, v.shape[1]), 1)
    v = jnp.where(cand, v, -jnp.inf)
    rank = jnp.zeros(v.shape, jnp.int32)
    for i in range(n_blocks):
        vi = v[:, i:i + 1]
        ahead = (vi > v) | ((vi == v) & (lane > i))
        rank = rank + ahead.astype(jnp.int32)
    return cand & (rank < n_top)


def _sample_attn_kernel(pt_ref, q_ref, *refs, mode, n_chunks, kpos0, n_new):
    kt_refs = refs[:n_chunks]
    vt_refs = refs[n_chunks:2 * n_chunks]
    knew_ref, vnew_ref, rs_ref, rp_ref = refs[2 * n_chunks:2 * n_chunks + 4]
    pos = 2 * n_chunks + 4
    if mode == "sel":
        mask_ref, e_ref = refs[pos], refs[pos + 1]
        pos += 2
    o_ref, st_ref = refs[pos], refs[pos + 1]
    del pt_ref
    q = q_ref[0]
    slope = rs_ref[...]
    qpos = rp_ref[...]
    lane = lax.broadcasted_iota(jnp.int32, (1, PAGE_SIZE), 1)
    n_past = n_chunks * PAGE_SIZE

    if mode == "sel":
        sel_bias = _tn(mask_ref[0], e_ref[...])
    if mode == "moba":
        per_blk = MOBA_BLOCK // PAGE_SIZE
        n_blk = n_chunks // per_blk
        kmean = jnp.zeros((q.shape[1], PAGE_SIZE), F32)
        for j in range(n_blk):
            tot = kt_refs[j * per_blk][0]
            for i in range(1, per_blk):
                tot = tot + kt_refs[j * per_blk + i][0]
            mean_j = jnp.sum(tot, axis=1, keepdims=True) * (1.0 / MOBA_BLOCK)
            kmean = jnp.where(lane == j, mean_j, kmean)
        keep = _rank_select_lanes(_nn(q, kmean.astype(BF16)), lane < qpos // MOBA_BLOCK, n_blk, MOBA_TOP)
        moba_bias = jnp.where(keep, 0.0, NEG_INF)

    for p in range(n_chunks):
        kpos = kpos0 + p * PAGE_SIZE + lane
        s = _nn(q, kt_refs[p][0].astype(BF16)) - slope * (qpos - kpos).astype(F32)
        if mode == "sel":
            s = s + sel_bias[:, p * PAGE_SIZE:(p + 1) * PAGE_SIZE]
        elif mode == "moba":
            s = s + moba_bias[:, p // per_blk:p // per_blk + 1]
        else:
            diff = qpos - kpos
            s = jnp.where((kpos >= 0) & (diff >= 0) & (diff < WINDOW), s, NEG_INF)
        st_ref[:, p * PAGE_SIZE:(p + 1) * PAGE_SIZE] = s
    kpos_new = kpos0 + n_past + lane
    s_new = _nn(q, knew_ref[0].astype(BF16)) - slope * (qpos - kpos_new).astype(F32)
    st_ref[:, n_past:] = jnp.where((lane < n_new) & (kpos_new <= qpos), s_new, NEG_INF)

    s_all = st_ref[...]
    e = jnp.exp(s_all - jnp.max(s_all, axis=1, keepdims=True))
    p_all = (e / jnp.sum(e, axis=1, keepdims=True)).astype(BF16)
    o = _nt(p_all[:, n_past:], vnew_ref[0].astype(BF16))
    for p in range(n_chunks):
        o = o + _nt(p_all[:, p * PAGE_SIZE:(p + 1) * PAGE_SIZE], vt_refs[p][0].astype(BF16))
    o_ref[0] = o


def sample_attn(mode, q_rows, kt_src, vt_src, page_table, kt_new, vt_new, row_slope, row_pos, kpos0, n_new,
                page0=0, win_chunks=0, mask=None, block_onehot=None):
    n_seq, rows, c = q_rows.shape
    if mode == "win":
        n_chunks = win_chunks
        src_spec = [pl.BlockSpec((1, c, PAGE_SIZE), functools.partial(lambda s, pt, p: (page0 + s, 0, p), p=p))
                    for p in range(n_chunks)]
    else:
        n_chunks = page_table.shape[1]
        src_spec = [pl.BlockSpec((1, c, PAGE_SIZE), functools.partial(
            lambda s, pt, p: (page0 + pt[s, p], 0, 0), p=p)) for p in range(n_chunks)]
    in_specs = [pl.BlockSpec((1, rows, c), lambda s, pt: (s, 0, 0))] + src_spec + src_spec
    in_specs += [pl.BlockSpec((1, c, PAGE_SIZE), lambda s, pt: (s, 0, 0)),
                 pl.BlockSpec((1, c, PAGE_SIZE), lambda s, pt: (s, 0, 0)),
                 pl.BlockSpec((rows, PAGE_SIZE), lambda s, pt: (0, 0)),
                 pl.BlockSpec((rows, PAGE_SIZE), lambda s, pt: (0, 0))]
    args = [q_rows] + [kt_src] * n_chunks + [vt_src] * n_chunks + [kt_new, vt_new, row_slope, row_pos]
    if mode == "sel":
        nb = mask.shape[1]
        in_specs += [pl.BlockSpec((1, nb, rows), lambda s, pt: (s, 0, 0)),
                     pl.BlockSpec(block_onehot.shape, lambda s, pt: (0, 0))]
        args += [mask, block_onehot]
    kern = functools.partial(_sample_attn_kernel, mode=mode, n_chunks=n_chunks, kpos0=kpos0, n_new=n_new)
    return pl.pallas_call(
        kern,
        grid_spec=pltpu.PrefetchScalarGridSpec(
            num_scalar_prefetch=1,
            grid=(n_seq,),
            in_specs=in_specs,
            out_specs=pl.BlockSpec((1, rows, c), lambda s, pt: (s, 0, 0)),
            scratch_shapes=[pltpu.VMEM((rows, (n_chunks + 1) * PAGE_SIZE), F32)]),
        out_shape=jax.ShapeDtypeStruct((n_seq, rows, c), F32),
        compiler_params=_cparams("parallel"),
        name="sample_attn_" + mode,
    )(page_table, *args)


def _compress_native_kernel(pt_ref, *refs, n_pages, n_groups):
    page_refs = refs[:n_pages]
    pe_ref, w1_ref, w2_ref, o_ref, stage_ref = refs[n_pages:]
    del pt_ref
    slot = pl.program_id(0) % SEQ_PER_BATCH
    page_rows = n_groups * HEAD_DIM
    for p in range(n_pages):
        start = pl.multiple_of((slot * n_pages + p) * page_rows, page_rows)
        stage_ref[pl.ds(start, page_rows), :] = page_refs[p][0]

    @pl.when(slot == SEQ_PER_BATCH - 1)
    def _():
        n_rows = SEQ_PER_BATCH * n_pages * n_groups
        hid = jnp.zeros((n_rows, PAGE_SIZE), F32)
        for dd in range(HEAD_DIM // 2):
            x0 = stage_ref[pl.ds(2 * dd, n_rows, stride=HEAD_DIM), :] + pe_ref[2 * dd:2 * dd + 1, :]
            x1 = stage_ref[pl.ds(2 * dd + 1, n_rows, stride=HEAD_DIM), :] + pe_ref[2 * dd + 1:2 * dd + 2, :]
            hid = hid + _nn(jnp.concatenate([x0, x1], axis=1).astype(BF16), w1_ref[dd])
        act = (hid * _sigmoid(hid)).astype(BF16)
        o_ref[...] = _nn(act, w2_ref[...])


def compress_native(pool_t, page_table, pe_t, w1_pairs, w2_blocks):
    n_seq, n_pages = page_table.shape
    page_rows = pool_t.shape[1]
    n_groups = page_rows // HEAD_DIM
    out_rows = SEQ_PER_BATCH * n_pages * n_groups
    in_specs = [pl.BlockSpec((1, page_rows, PAGE_SIZE), functools.partial(lambda s, pt, p: (pt[s, p], 0, 0), p=p))
                for p in range(n_pages)]
    in_specs += [pl.BlockSpec(pe_t.shape, lambda s, pt: (0, 0)),
                 pl.BlockSpec(w1_pairs.shape, lambda s, pt: (0, 0, 0)),
                 pl.BlockSpec(w2_blocks.shape, lambda s, pt: (0, 0))]
    return pl.pallas_call(
        functools.partial(_compress_native_kernel, n_pages=n_pages, n_groups=n_groups),
        grid_spec=pltpu.PrefetchScalarGridSpec(
            num_scalar_prefetch=1,
            grid=(n_seq,),
            in_specs=in_specs,
            out_specs=pl.BlockSpec((out_rows, PAGE_SIZE), lambda s, pt: (s // SEQ_PER_BATCH, 0)),
            scratch_shapes=[pltpu.VMEM((SEQ_PER_BATCH * n_pages * page_rows, PAGE_SIZE), F32)]),
        out_shape=jax.ShapeDtypeStruct((n_seq * n_pages * n_groups, PAGE_SIZE), F32),
        compiler_params=_cparams("arbitrary"),
        name="compress_native",
    )(page_table, *([pool_t] * n_pages), pe_t, w1_pairs, w2_blocks)


NSA_KV_W = NSA_KV_HEADS * HEAD_DIM
NSA_PROJ_W = 3840
NSA_Z_BLK, NSA_KV_BLK0, NSA_GL_BLK = 1, 8, 14


def _heads_major(t2d, batch, n_heads):
    n = t2d.shape[0] // batch
    return t2d.reshape(batch, n, n_heads, HEAD_DIM).transpose(0, 2, 1, 3).astype(BF16)


def _heads_major_v(t2d, batch, n_heads):
    v = _heads_major(t2d, batch, n_heads)
    ones = jnp.ones(v.shape[:-1] + (1,), BF16)
    return jnp.concatenate([v, ones, jnp.zeros(v.shape[:-1] + (LANES - HEAD_DIM - 1,), BF16)], axis=-1)


def _block_onehot(seq, block):
    return (jnp.arange(seq)[:, None] // block == jnp.arange(seq // block)[None, :]).astype(BF16)


def _column_tables(slopes, past, n_tok):
    col = jnp.arange(COLS)
    live = col < N_HEADS * n_tok
    head = jnp.minimum(col // n_tok, N_HEADS - 1)
    slope = jnp.where(live, slopes[head], 0.0)
    pos = jnp.where(live, past + col % n_tok, past).astype(jnp.int32)
    grp = jnp.where(live, head // NSA_GROUP, -1).astype(jnp.int32)
    ints = jnp.concatenate([pos[None, :], grp[None, :], jnp.zeros((6, COLS), jnp.int32)], axis=0)
    return jnp.broadcast_to(slope[None, :], (8, COLS)), ints


def _row_tables(col_slope, col_ints):
    return (jnp.broadcast_to(col_slope[0][:, None], (COLS, PAGE_SIZE)),
            jnp.broadcast_to(col_ints[0][:, None], (COLS, PAGE_SIZE)))


def _new_rows_t(t2d, n_seq, n_tok):
    rows_t = t2d.reshape(n_seq, n_tok, -1).transpose(0, 2, 1)
    return jnp.pad(rows_t, ((0, 0), (0, 0), (0, PAGE_SIZE - n_tok)))


def _stored_tiles(cache):
    nd = cache.ndim
    t = jnp.transpose(cache, tuple(range(nd - 3)) + (nd - 2, nd - 1, nd - 3))
    return t.reshape((-1, cache.shape[-2] * cache.shape[-1], cache.shape[-3]))


def _expand_query(q2d, n_seq, n_tok, heads_per_key):
    q4 = (q2d * ATTN_SCALE).reshape(n_seq, n_tok, N_HEADS, HEAD_DIM)
    n_keys = N_HEADS // heads_per_key
    owner = (jnp.arange(N_HEADS)[:, None] // heads_per_key == jnp.arange(n_keys)[None, :])
    qbd = jnp.where(owner[None, None, :, None, :], q4[..., None], 0.0)
    qbd = qbd.transpose(0, 4, 3, 2, 1).reshape(n_seq, n_keys * HEAD_DIM, N_HEADS * n_tok)
    return jnp.pad(qbd, ((0, 0), (0, 0), (0, COLS - N_HEADS * n_tok))).astype(BF16)


def _take_own(o_t, n_seq, n_tok, heads_per_key):
    n_keys = N_HEADS // heads_per_key
    o5 = o_t[:, :N_HEADS * n_tok].reshape(n_seq, N_HEADS, n_tok, n_keys, HEAD_DIM)
    hh = jnp.arange(N_HEADS)
    own = o5[:, hh, :, hh // heads_per_key, :]
    return own.transpose(1, 2, 0, 3).reshape(n_seq * n_tok, N_HEADS * HEAD_DIM)


def _pad_rows(t2d, n_seq, n_tok):
    return jnp.pad(t2d.reshape(n_seq, n_tok, -1), ((0, 0), (0, 8 - n_tok), (0, 0)))


def _nsa_weights(w_in, pe_k, pe_v, w1_k, w2_k, w1_v, w2_v):
    d = w_in.shape[0]
    q, kv, gl, z = (w_in[:, :1024], w_in[:, 1024:1024 + 6 * NSA_KV_W],
                    w_in[:, 1024 + 6 * NSA_KV_W:1024 + 6 * NSA_KV_W + 3 * N_HEADS], w_in[:, -1024:])
    pad = jnp.zeros((d, NSA_PROJ_W - (2048 + 6 * NSA_KV_W + 3 * N_HEADS)), w_in.dtype)
    w_re = jnp.concatenate([q, z, kv, gl, pad], axis=1).astype(BF16)
    eye = jnp.eye(NSA_KV_HEADS, dtype=w1_k.dtype)

    def big(pe, w1, w2):
        hid = w1.shape[1]
        w1r = w1.reshape(CMP_BLOCK, HEAD_DIM, hid)
        w1b = (w1r[:, None, :, None, :] * eye[None, :, None, :, None]).reshape(
            CMP_BLOCK * NSA_KV_W, NSA_KV_HEADS * hid).astype(BF16)
        w2b = (w2[None, :, None, :] * eye[:, None, :, None]).reshape(
            NSA_KV_HEADS * hid, NSA_KV_W).astype(BF16)
        peb = jnp.broadcast_to(pe[:, None, :], (CMP_BLOCK, NSA_KV_HEADS, HEAD_DIM)).reshape(1, -1)
        return peb, w1b, w2b

    bpp = PAGE_SIZE // CMP_BLOCK
    eye_b = jnp.eye(bpp, dtype=w1_k.dtype)

    def stored(pe, w1, w2):
        hid = w1.shape[1]
        w1r = w1.reshape(CMP_BLOCK, HEAD_DIM, hid).transpose(1, 0, 2)
        w1d = (w1r[:, None, :, None, :] * eye_b[None, :, None, :, None]).reshape(
            HEAD_DIM, bpp * CMP_BLOCK, bpp * hid)
        w1p = w1d.reshape(HEAD_DIM // 2, 2 * bpp * CMP_BLOCK, bpp * hid).astype(BF16)
        w2b = (w2[None, :, None, :] * eye_b[:, None, :, None]).reshape(bpp * hid, bpp * HEAD_DIM).astype(BF16)
        return jnp.tile(pe.T, (1, bpp)), w1p, w2b

    return (w_re, big(pe_k, w1_k, w2_k), big(pe_v, w1_v, w2_v), stored(pe_k, w1_k, w2_k),
            stored(pe_v, w1_v, w2_v))


def _nsa_prompt(x, nw, w_re, cmp_k, cmp_v, w_out, slopes, batch, final_w):
    t = x.shape[0]
    n = t // batch
    proj = norm_matmul(x, nw, w_re)
    kv = [proj[:, 2048 + NSA_KV_W * i:2048 + NSA_KV_W * (i + 1)] for i in range(6)]
    ck, cv, sk, sv, wk, wv = kv
    row_w = CMP_BLOCK * NSA_KV_W
    bpp = PAGE_SIZE // CMP_BLOCK
    n_pages = t // PAGE_SIZE
    pt = jnp.arange(n_pages, dtype=jnp.int32).reshape(SEQ_PER_BATCH, n_pages // SEQ_PER_BATCH)
    nb = n // CMP_BLOCK
    comp = lambda rows, wts: compress_paged(rows.reshape(n_pages, bpp, row_w), pt, *wts).reshape(
        batch, nb, NSA_KV_HEADS, HEAD_DIM).transpose(0, 2, 1, 3).astype(BF16)
    o_cmp, sel_t = nsa_cmp_prompt(proj, comp(ck, cmp_k), comp(cv, cmp_v), slopes, batch)
    o_sel = flash_prompt(proj, 0, _heads_major(sk, batch, NSA_KV_HEADS), _heads_major_v(sv, batch, NSA_KV_HEADS),
                         slopes, batch, mask=sel_t.transpose(0, 1, 3, 2), et=_block_onehot(n, CMP_BLOCK),
                         tq=128, tk=512)
    o_win = flash_prompt(proj, 0, _heads_major(wk, batch, NSA_KV_HEADS), _heads_major_v(wv, batch, NSA_KV_HEADS),
                         slopes, batch, window=WINDOW, tq=128, tk=256)
    y = out_proj(x, proj, NSA_Z_BLK, [o_cmp, o_sel, o_win], w_out, gl_blk=NSA_GL_BLK, final_w=final_w)
    st = lambda a: a.reshape(batch, n, NSA_KV_HEADS, HEAD_DIM)
    keep = min(WINDOW, n)
    return y, (st(ck), st(cv), st(sk), st(sv), st(wk)[:, n - keep:], st(wv)[:, n - keep:])


def _nsa_sample(x, nw, w_re, cmp_k, cmp_v, w_out, slopes, n_seq, final_w, page_table, layer_j, caches):
    ck_t, cv_t, sk_t, sv_t, wk_t, wv_t, win_k, win_v, n_pool = caches
    n_tok = x.shape[0] // n_seq
    n_pages = page_table.shape[1]
    past = n_pages * PAGE_SIZE
    win_len = win_k.shape[2]
    bpp = PAGE_SIZE // CMP_BLOCK
    proj = norm_matmul(x, nw, w_re)
    kv = [proj[:, 2048 + NSA_KV_W * i:2048 + NSA_KV_W * (i + 1)] for i in range(6)]
    ck, cv, sk, sv, wk, wv = kv
    pt_layer = page_table + layer_j * n_pool
    comp = lambda pool_t, wts: compress_native(pool_t, pt_layer, *wts).reshape(
        n_seq, n_pages, NSA_KV_HEADS, bpp, HEAD_DIM).transpose(0, 1, 3, 2, 4).reshape(
        n_seq, n_pages * bpp, NSA_KV_W)
    col_slope, col_ints = _column_tables(slopes, past, n_tok)
    qbd = _expand_query(proj[:, :1024], n_seq, n_tok, NSA_GROUP)
    o_cmp, mask = nsa_sample_cmp(qbd, comp(ck_t, cmp_k), comp(cv_t, cmp_v), col_slope, col_ints)
    q_rows = qbd.transpose(0, 2, 1)
    row_slope, row_pos = _row_tables(col_slope, col_ints)
    new_t = lambda a: _new_rows_t(a, n_seq, n_tok)
    block_onehot = (jnp.arange(past // CMP_BLOCK)[:, None] == jnp.arange(past)[None, :] // CMP_BLOCK).astype(BF16)
    o_sel = sample_attn("sel", q_rows, sk_t, sv_t, page_table, new_t(sk), new_t(sv), row_slope, row_pos, 0, n_tok,
                        page0=layer_j * n_pool, mask=mask, block_onehot=block_onehot)
    o_win = sample_attn("win", q_rows, wk_t, wv_t, page_table, new_t(wk), new_t(wv), row_slope, row_pos,
                        past - win_len, n_tok, page0=layer_j * n_seq, win_chunks=win_len // PAGE_SIZE)
    own = lambda o: _take_own(o, n_seq, n_tok, NSA_GROUP)
    y = out_proj(x, proj, NSA_Z_BLK, [own(o_cmp), own(o_sel), own(o_win)], w_out, gl_blk=NSA_GL_BLK,
                 final_w=final_w)
    st = lambda a: a.reshape(n_seq, n_tok, NSA_KV_HEADS, HEAD_DIM)
    kw = jnp.concatenate([win_k[layer_j], st(wk)], axis=1)
    vw = jnp.concatenate([win_v[layer_j], st(wv)], axis=1)
    keep = min(WINDOW, kw.shape[1])
    return y, (st(ck), st(cv), st(sk), st(sv), kw[:, kw.shape[1] - keep:], vw[:, vw.shape[1] - keep:])


def _nsa_cache_views(cmp_k, cmp_v, sel_k, sel_v, win_k, win_v):
    n_pool = cmp_k.shape[1]
    return (_stored_tiles(cmp_k), _stored_tiles(cmp_v), _stored_tiles(sel_k), _stored_tiles(sel_v),
            _stored_tiles(win_k), _stored_tiles(win_v), win_k, win_v, n_pool)


def _moba_prompt(x, nw, w_in, w_out, slopes, batch, final_w):
    t = x.shape[0]
    n = t // batch
    proj = norm_matmul(x, nw, w_in)
    k, v = proj[:, 1024:2048], proj[:, 2048:3072]
    nf = n // MOBA_BLOCK
    kmean = block_mean(proj, 1, MOBA_BLOCK).reshape(batch, nf, N_HEADS, HEAD_DIM).transpose(0, 2, 1, 3).astype(BF16)
    sel_t = moba_select(proj, kmean, batch)
    o = flash_prompt(proj, 0, _heads_major(k, batch, N_HEADS), _heads_major_v(v, batch, N_HEADS), slopes, batch,
                     mask=sel_t.transpose(0, 1, 3, 2), et=_block_onehot(n, MOBA_BLOCK), tq=512, tk=512)
    y = out_proj(x, proj, 3, [o], w_out, final_w=final_w)
    st = lambda a: a.reshape(batch, n, N_HEADS, HEAD_DIM)
    return y, (st(k), st(v))


def _moba_sample(x, nw, w_in, w_out, slopes, n_seq, final_w, page_table, k_pool, v_pool):
    n_tok = x.shape[0] // n_seq
    past = page_table.shape[1] * PAGE_SIZE
    assert past % MOBA_BLOCK == 0 and n_tok <= MOBA_BLOCK
    proj = norm_matmul(x, nw, w_in)
    k, v = proj[:, 1024:2048], proj[:, 2048:3072]
    q_rows = _expand_query(proj[:, :1024], n_seq, n_tok, 1).transpose(0, 2, 1)
    row_slope, row_pos = _row_tables(*_column_tables(slopes, past, n_tok))
    o = sample_attn("moba", q_rows, _stored_tiles(k_pool), _stored_tiles(v_pool), page_table,
                    _new_rows_t(k, n_seq, n_tok), _new_rows_t(v, n_seq, n_tok), row_slope, row_pos, 0, n_tok)
    y = out_proj(x, proj, 3, [_take_own(o, n_seq, n_tok, 1)], w_out, final_w=final_w)
    st = lambda a: a.reshape(n_seq, n_tok, N_HEADS, HEAD_DIM)
    return y, (st(k), st(v))


def _ret_prompt_layer(x, nw, w_in, gn_w, gn_b, w_out, batch, final_w):
    proj = norm_matmul(x, nw, w_in)
    on, state = ret_prompt(proj, batch, gn_w, gn_b)
    return out_proj(x, proj, 3, [on], w_out, final_w=final_w), state


def _ret_sample_layer(x, nw, w_in, gn_w, gn_b, w_out, n_seq, final_w, state):
    n_tok = x.shape[0] // n_seq
    proj = norm_matmul(x, nw, w_in)
    on, new_state = ret_sample(_pad_rows(proj, n_seq, n_tok), state, gn_w, gn_b, n_tok)
    on = on[:, :n_tok].reshape(n_seq * n_tok, -1)
    return out_proj(x, proj, 3, [on], w_out, final_w=final_w), new_state


def kernel(x_prompt, x_sample, cache_nsa_cmp_k, cache_nsa_cmp_v, cache_nsa_sel_k, cache_nsa_sel_v,
           cache_nsa_win_k, cache_nsa_win_v, cache_moba_k, cache_moba_v, state_ret, page_table,
           norm_w, final_norm_w, nsa_w_in, nsa_pe_k, nsa_pe_v, nsa_w1_k, nsa_w2_k, nsa_w1_v, nsa_w2_v,
           nsa_w_out, moba_w_in, moba_w_out, ret_w_in, ret_gn_w, ret_gn_b, ret_w_out):
    batch, seq, d = x_prompt.shape
    n_seq, n_tok, _ = x_sample.shape
    depth = norm_w.shape[0]
    slopes = jnp.exp2(-8.0 * (jnp.arange(N_HEADS, dtype=F32) + 1.0) / N_HEADS)
    xp = x_prompt.reshape(batch * seq, d)
    xs = x_sample.reshape(n_seq * n_tok, d)
    nsa_p, nsa_s, moba_p, moba_s, ret_p, ret_s = [], [], [], [], [], []
    nsa_caches = _nsa_cache_views(cache_nsa_cmp_k, cache_nsa_cmp_v, cache_nsa_sel_k, cache_nsa_sel_v,
                                  cache_nsa_win_k, cache_nsa_win_v)
    for layer in range(depth):
        j = layer // N_MIXERS
        fw = final_norm_w if layer == depth - 1 else None
        nw = norm_w[layer]
        if layer % N_MIXERS == 0:
            w_re, cmp_k, cmp_v, cmp_k_st, cmp_v_st = _nsa_weights(
                nsa_w_in[j], nsa_pe_k[j], nsa_pe_v[j], nsa_w1_k[j], nsa_w2_k[j], nsa_w1_v[j], nsa_w2_v[j])
            w_out = nsa_w_out[j].astype(BF16)
            xp, stp = _nsa_prompt(xp, nw, w_re, cmp_k, cmp_v, w_out, slopes, batch, fw)
            xs, sts = _nsa_sample(xs, nw, w_re, cmp_k_st, cmp_v_st, w_out, slopes, n_seq, fw, page_table, j,
                                  nsa_caches)
            nsa_p.append(stp)
            nsa_s.append(sts)
        elif layer % N_MIXERS == 1:
            w_in, w_out = moba_w_in[j].astype(BF16), moba_w_out[j].astype(BF16)
            xp, stp = _moba_prompt(xp, nw, w_in, w_out, slopes, batch, fw)
            xs, sts = _moba_sample(xs, nw, w_in, w_out, slopes, n_seq, fw, page_table, cache_moba_k[j],
                                   cache_moba_v[j])
            moba_p.append(stp)
            moba_s.append(sts)
        else:
            w_in, w_out = ret_w_in[j].astype(BF16), ret_w_out[j].astype(BF16)
            xp, stp = _ret_prompt_layer(xp, nw, w_in, ret_gn_w[j], ret_gn_b[j], w_out, batch, fw)
            xs, sts = _ret_sample_layer(xs, nw, w_in, ret_gn_w[j], ret_gn_b[j], w_out, n_seq, fw, state_ret[j])
            ret_p.append(stp)
            ret_s.append(sts)
    st = lambda items, i: jnp.stack([s[i] for s in items])
    return (xp.reshape(batch, seq, d), xs.reshape(n_seq, n_tok, d),
            st(nsa_p, 0), st(nsa_p, 1), st(nsa_p, 2), st(nsa_p, 3), st(nsa_p, 4), st(nsa_p, 5),
            st(moba_p, 0), st(moba_p, 1), jnp.stack(ret_p),
            st(nsa_s, 0), st(nsa_s, 1), st(nsa_s, 2), st(nsa_s, 3), st(nsa_s, 4), st(nsa_s, 5),
            st(moba_s, 0), st(moba_s, 1), jnp.stack(ret_s))
```

```python
import functools

import jax
import jax.numpy as jnp
import numpy as np
from jax import lax
from jax.experimental import pallas as pl
from jax.experimental.pallas import tpu as pltpu

F32 = jnp.float32
BF16 = jnp.bfloat16

HEAD_DIM = 64
N_HEADS = 16
NSA_KV_HEADS = 4
NSA_GROUP = 4
CMP_BLOCK = 64
SEL_TOP = 15
WINDOW = 512
MOBA_BLOCK = 256
MOBA_TOP = 3
RET_HEADS = 4
RET_DK = 256
RET_CHUNK = 128
PAGE_SIZE = 128
RMS_EPS = 1e-6
GN_EPS = 1e-5
NEG_INF = -1e30
ATTN_SCALE = HEAD_DIM ** -0.5
N_MIXERS = 3

VMEM_LIMIT = 56 * 1024 * 1024


def _cparams(*sem):
    return pltpu.CompilerParams(dimension_semantics=sem, vmem_limit_bytes=VMEM_LIMIT)


def _nt(a, b):
    return lax.dot_general(a, b, (((1,), (1,)), ((), ())), preferred_element_type=F32)


def _nn(a, b):
    return lax.dot_general(a, b, (((1,), (0,)), ((), ())), preferred_element_type=F32)


def _tn(a, b):
    return lax.dot_general(a, b, (((0,), (0,)), ((), ())), preferred_element_type=F32)


def _sigmoid(x):
    return 1.0 / (1.0 + jnp.exp(-x))


def _norm_matmul_kernel(x_ref, nw_ref, w_ref, o_ref, xn_ref):
    @pl.when(pl.program_id(1) == 0)
    def _():
        x = x_ref[...]
        ms = jnp.mean(x * x, axis=-1, keepdims=True)
        xn_ref[...] = ((x * lax.rsqrt(ms + RMS_EPS)) * nw_ref[...]).astype(BF16)

    o_ref[...] = _nn(xn_ref[...], w_ref[...])


def norm_matmul(x, nw, w_bf16):
    t, d = x.shape
    n = w_bf16.shape[1]
    tm = min(t, 1024)
    tn = 1024 if n % 1024 == 0 else 768
    assert n % tn == 0
    return pl.pallas_call(
        _norm_matmul_kernel,
        grid=(t // tm, n // tn),
        in_specs=[pl.BlockSpec((tm, d), lambda i, j: (i, 0)),
                  pl.BlockSpec((1, d), lambda i, j: (0, 0)),
                  pl.BlockSpec((d, tn), lambda i, j: (0, j))],
        out_specs=pl.BlockSpec((tm, tn), lambda i, j: (i, j)),
        out_shape=jax.ShapeDtypeStruct((t, n), F32),
        scratch_shapes=[pltpu.VMEM((tm, d), BF16)],
        compiler_params=_cparams("parallel", "arbitrary"),
        name="norm_matmul",
    )(x, nw.reshape(1, d), w_bf16)


def _out_proj_kernel(*refs, n_o, nsa_gates, final_norm):
    x_ref, z_ref = refs[0], refs[1]
    o_refs = refs[2:2 + n_o]
    pos = 2 + n_o
    if nsa_gates:
        gl_ref = refs[pos]
        pos += 1
    w_ref = refs[pos]
    pos += 1
    if final_norm:
        fw_ref = refs[pos]
        pos += 1
    y_ref = refs[pos]

    if nsa_gates:
        gates = _sigmoid(gl_ref[...])
        tm = gates.shape[0]
        lane = lax.broadcasted_iota(jnp.int32, (tm, 128), 1)
        cols = []
        for vb in range(N_HEADS // 2):
            acc = None
            for br in range(n_o):
                c0 = (2 * vb) * 3 + br
                c1 = (2 * vb + 1) * 3 + br
                g = jnp.where(lane < HEAD_DIM,
                              jnp.broadcast_to(gates[:, c0:c0 + 1], (tm, 128)),
                              jnp.broadcast_to(gates[:, c1:c1 + 1], (tm, 128)))
                term = g * o_refs[br][:, vb * 128:(vb + 1) * 128]
                acc = term if acc is None else acc + term
            cols.append(acc)
        o = jnp.concatenate(cols, axis=1)
    else:
        o = o_refs[0][...]
    z = z_ref[...]
    gated = (o * (z * _sigmoid(z))).astype(BF16)
    y = x_ref[...] + _nn(gated, w_ref[...])
    if final_norm:
        ms = jnp.mean(y * y, axis=-1, keepdims=True)
        y = (y * lax.rsqrt(ms + RMS_EPS)) * fw_ref[...]
    y_ref[...] = y


def out_proj(x, proj, z_blk, o_list, w_bf16, gl_blk=None, final_w=None, tm=256):
    t, d = x.shape
    tm = min(tm, t)
    n_o = len(o_list)
    in_specs = [pl.BlockSpec((tm, d), lambda i: (i, 0)),
                pl.BlockSpec((tm, d), lambda i: (i, z_blk))]
    args = [x, proj]
    for o in o_list:
        in_specs.append(pl.BlockSpec((tm, d), lambda i: (i, 0)))
        args.append(o)
    if gl_blk is not None:
        in_specs.append(pl.BlockSpec((tm, 256), lambda i: (i, gl_blk)))
        args.append(proj)
    in_specs.append(pl.BlockSpec((d, d), lambda i: (0, 0)))
    args.append(w_bf16)
    if final_w is not None:
        in_specs.append(pl.BlockSpec((1, d), lambda i: (0, 0)))
        args.append(final_w.reshape(1, d))
    kern = functools.partial(_out_proj_kernel, n_o=n_o, nsa_gates=gl_blk is not None,
                             final_norm=final_w is not None)
    return pl.pallas_call(
        kern,
        grid=(t // tm,),
        in_specs=in_specs,
        out_specs=pl.BlockSpec((tm, d), lambda i: (i, 0)),
        out_shape=jax.ShapeDtypeStruct((t, d), F32),
        compiler_params=_cparams("parallel"),
        name="out_proj",
    )(*args)


def _ret_tables(c_true, c_pad):
    lg = jnp.log1p(-jnp.exp2(-5.0 - jnp.arange(RET_HEADS, dtype=F32)))[:, None, None]
    i = jnp.arange(c_pad, dtype=F32)
    live = (i < c_true)
    diff = i[:, None] - i[None, :]
    dmat = jnp.where((diff >= 0) & live[:, None] & live[None, :], jnp.exp(lg * jnp.maximum(diff, 0.0)), 0.0)
    qdec = jnp.exp(lg * (i[:, None] + 1.0)) * jnp.ones((1, 1, 128), F32)
    kdec = jnp.where(live[:, None], jnp.exp(lg * (c_true - 1.0 - i[:, None])), 0.0) * jnp.ones((1, 1, 128), F32)
    sdec = jnp.exp(lg * c_true) * jnp.ones((1, 8, 128), F32)
    return dmat, qdec, kdec, sdec


def _ret_chunk(q_ref, k_ref, v_ref, d_ref, qd_ref, kd_ref, sd_ref, gw_ref, gb_ref, o_ref, st_ref):
    shp = q_ref.shape
    q = q_ref[...].reshape(shp[-2], shp[-1])
    k = k_ref[...].reshape(shp[-2], shp[-1]) * (RET_DK ** -0.5)
    v = v_ref[...].reshape(shp[-2], shp[-1])
    qb, kb, vb = q.astype(BF16), k.astype(BF16), v.astype(BF16)
    state = st_ref[...]
    dmat = d_ref[0]
    qdec = qd_ref[0][:, 0:1]
    kdec = kd_ref[0][:, 0:1]
    sdec = sd_ref[0][0:1, 0:1]
    inner = _nt(qb, kb) * dmat
    o = _nn(inner.astype(BF16), vb) + _nn(qb, state.astype(BF16)) * qdec
    kw = (k * kdec).astype(BF16)
    new_state = sdec * state + _nn(kw.T, vb)

    mu = jnp.mean(o, axis=-1, keepdims=True)
    var = jnp.mean(jnp.square(o - mu), axis=-1, keepdims=True)
    on = (o - mu) * lax.rsqrt(var + GN_EPS)
    o_ref[...] = (on * gw_ref[...] + gb_ref[...]).reshape(o_ref.shape)
    return new_state


def _ret_prompt_kernel(q_ref, k_ref, v_ref, d_ref, qd_ref, kd_ref, sd_ref, gw_ref, gb_ref,
                       o_ref, sn_ref, st_ref, *, n_chunks):
    c = pl.program_id(2)

    @pl.when(c == 0)
    def _():
        st_ref[...] = jnp.zeros_like(st_ref)

    new_state = _ret_chunk(q_ref, k_ref, v_ref, d_ref, qd_ref, kd_ref, sd_ref, gw_ref, gb_ref, o_ref, st_ref)
    st_ref[...] = new_state

    @pl.when(c == n_chunks - 1)
    def _():
        sn_ref[0, 0] = new_state


def _ret_sample_kernel(q_ref, k_ref, v_ref, s0_ref, d_ref, qd_ref, kd_ref, sd_ref, gw_ref, gb_ref,
                       o_ref, sn_ref, st_ref):
    st_ref[...] = s0_ref[0, 0]
    sn_ref[0, 0] = _ret_chunk(q_ref, k_ref, v_ref, d_ref, qd_ref, kd_ref, sd_ref, gw_ref, gb_ref, o_ref, st_ref)


def ret_prompt(proj, batch, gn_w, gn_b):
    t = proj.shape[0]
    n = t // batch
    nc = n // RET_CHUNK
    c = RET_CHUNK
    dmat, qdec, kdec, sdec = _ret_tables(c, c)
    hh = RET_HEADS
    tab = lambda r: pl.BlockSpec((1, r, 128), lambda b, h, j: (h, 0, 0))
    return pl.pallas_call(
        functools.partial(_ret_prompt_kernel, n_chunks=nc),
        grid=(batch, hh, nc),
        in_specs=[pl.BlockSpec((c, 256), lambda b, h, j: (b * nc + j, h)),
                  pl.BlockSpec((c, 256), lambda b, h, j: (b * nc + j, hh + h)),
                  pl.BlockSpec((c, 256), lambda b, h, j: (b * nc + j, 2 * hh + h)),
                  pl.BlockSpec((1, c, c), lambda b, h, j: (h, 0, 0)),
                  tab(c), tab(c), tab(8),
                  pl.BlockSpec((1, 256), lambda b, h, j: (0, h)),
                  pl.BlockSpec((1, 256), lambda b, h, j: (0, h))],
        out_specs=[pl.BlockSpec((c, 256), lambda b, h, j: (b * nc + j, h)),
                   pl.BlockSpec((1, 1, RET_DK, 256), lambda b, h, j: (b, h, 0, 0))],
        out_shape=[jax.ShapeDtypeStruct((t, 1024), F32),
                   jax.ShapeDtypeStruct((batch, hh, RET_DK, 256), F32)],
        scratch_shapes=[pltpu.VMEM((RET_DK, 256), F32)],
        compiler_params=_cparams("parallel", "parallel", "arbitrary"),
        name="ret_prompt",
    )(proj, proj, proj, dmat, qdec, kdec, sdec, gn_w.reshape(1, -1), gn_b.reshape(1, -1))


def ret_sample(proj3, state, gn_w, gn_b, n_tok):
    b, cp = proj3.shape[0], proj3.shape[1]
    dmat, qdec, kdec, sdec = _ret_tables(n_tok, cp)
    hh = RET_HEADS
    tab = lambda r: pl.BlockSpec((1, r, 128), lambda s, h: (h, 0, 0))
    return pl.pallas_call(
        _ret_sample_kernel,
        grid=(b, hh),
        in_specs=[pl.BlockSpec((1, cp, 256), lambda s, h: (s, 0, h)),
                  pl.BlockSpec((1, cp, 256), lambda s, h: (s, 0, hh + h)),
                  pl.BlockSpec((1, cp, 256), lambda s, h: (s, 0, 2 * hh + h)),
                  pl.BlockSpec((1, 1, RET_DK, 256), lambda s, h: (s, h, 0, 0)),
                  pl.BlockSpec((1, cp, cp), lambda s, h: (h, 0, 0)),
                  tab(cp), tab(cp), tab(8),
                  pl.BlockSpec((1, 256), lambda s, h: (0, h)),
                  pl.BlockSpec((1, 256), lambda s, h: (0, h))],
        out_specs=[pl.BlockSpec((1, cp, 256), lambda s, h: (s, 0, h)),
                   pl.BlockSpec((1, 1, RET_DK, 256), lambda s, h: (s, h, 0, 0))],
        out_shape=[jax.ShapeDtypeStruct((b, cp, 1024), F32),
                   jax.ShapeDtypeStruct((b, hh, RET_DK, 256), F32)],
        scratch_shapes=[pltpu.VMEM((RET_DK, 256), F32)],
        compiler_params=_cparams("parallel", "parallel"),
        name="ret_sample",
    )(proj3, proj3, proj3, state, dmat, qdec, kdec, sdec, gn_w.reshape(1, -1), gn_b.reshape(1, -1))


HB = 4


LANES = 128


LOG2E = 1.4426950408889634
ALIBI_COLS = 6


def _flash_kernel(q_ref, k_ref, v_ref, o_ref, m_ref, acc_ref, *, tq, tk, n_stack, window):
    q0 = pl.program_id(2) * tq
    n_loop = HB // n_stack
    kd = k_ref.shape[-1]
    qpos = q0 + lax.broadcasted_iota(jnp.int32, (1, tq, 1), 1)
    lo = jnp.maximum(q0 - window + 1, 0) // tk if window else 0
    hi = (q0 + tq + tk - 1) // tk

    for i in range(n_loop):
        heads = [i * n_stack + r for r in range(n_stack)]
        q_st = jnp.concatenate([q_ref[:, h * kd:(h + 1) * kd] for h in heads], axis=0)
        m_ref[...] = jnp.full(m_ref.shape, NEG_INF, F32)
        acc_ref[...] = jnp.zeros(acc_ref.shape, F32)

        def tile(j, edge):
            k0 = pl.multiple_of(j * tk, tk)
            s = _nt(q_st, k_ref[0, i, pl.ds(k0, tk), :])
            if edge:
                kpos = k0 + lax.broadcasted_iota(jnp.int32, (1, 1, tk), 2)
                ok = kpos <= qpos
                if window:
                    ok = ok & ((qpos - kpos) < window)
                s = jnp.where(ok, s.reshape(n_stack, tq, tk), NEG_INF).reshape(n_stack * tq, tk)
            m_prev = m_ref[...]
            m_new = jnp.maximum(m_prev, jnp.max(s, axis=-1, keepdims=True))
            p = jnp.exp2(s - jnp.concatenate([m_new] * (tk // LANES), axis=1))
            acc_ref[...] = (jnp.exp2(m_prev - m_new) * acc_ref[...]
                            + _nn(p.astype(BF16), v_ref[0, i, pl.ds(k0, tk), :]))
            m_ref[...] = m_new

        def edge_body(j, carry):
            tile(j, True)
            return carry

        def inner_body(j, carry):
            tile(j, False)
            return carry

        if window:
            lax.fori_loop(lo, hi, edge_body, 0)
        else:
            lax.fori_loop(lo, hi - 1, inner_body, 0)
            tile(hi - 1, True)
        acc = acc_ref[...]
        o = acc[:, :HEAD_DIM] / acc[:, HEAD_DIM:HEAD_DIM + 1]
        for r, h in enumerate(heads):
            o_ref[:, h * HEAD_DIM:(h + 1) * HEAD_DIM] = o[r * tq:(r + 1) * tq]


def flash_prompt(q_aug, k_aug, vh, batch, window=0, tq=128, tk=512):
    seq, kd = k_aug.shape[2], k_aug.shape[3]
    assert tk % tq == 0 or window, "the causal edge is taken to be the last key tile only"
    n_groups = N_HEADS // HB
    nq = seq // tq
    n_stack = HB if k_aug.shape[1] == n_groups else 1
    kvb = HB // n_stack
    rows = n_stack * tq
    kern = functools.partial(_flash_kernel, tq=tq, tk=tk, n_stack=n_stack, window=window)
    return pl.pallas_call(
        kern,
        grid=(batch, n_groups, nq),
        in_specs=[pl.BlockSpec((tq, HB * kd), lambda b, g, i: (b * nq + i, g)),
                  pl.BlockSpec((1, kvb, seq, kd), lambda b, g, i: (b, g, 0, 0)),
                  pl.BlockSpec((1, kvb, seq, LANES), lambda b, g, i: (b, g, 0, 0))],
        out_specs=pl.BlockSpec((tq, HB * HEAD_DIM), lambda b, g, i: (b * nq + i, g)),
        out_shape=jax.ShapeDtypeStruct((batch * seq, N_HEADS * HEAD_DIM), F32),
        scratch_shapes=[pltpu.VMEM((rows, LANES), F32), pltpu.VMEM((rows, LANES), F32)],
        compiler_params=_cparams("parallel", "parallel", "arbitrary"),
        name="flash_prompt",
    )(q_aug, k_aug, vh)


def _rank_select(v, cand, blk, n_blocks, n_top):
    v = jnp.where(cand, v, -jnp.inf)
    rank = jnp.zeros(v.shape, jnp.int32)
    for i in range(n_blocks):
        vi = v[i:i + 1, :]
        ahead = (vi > v) | ((vi == v) & (blk > i))
        rank = rank + ahead.astype(jnp.int32)
    return cand & (rank < n_top)


def _moba_select_kernel(q_ref, km_ref, o_ref, *, tq, n_blocks):
    q0 = pl.program_id(2) * tq
    own = (q0 + lax.broadcasted_iota(jnp.int32, (1, tq), 1)) // MOBA_BLOCK
    blk = lax.broadcasted_iota(jnp.int32, (n_blocks, 1), 0)
    for r in range(HB):
        q = q_ref[:, r * HEAD_DIM:(r + 1) * HEAD_DIM].astype(BF16)
        score = _nt(km_ref[0, r], q)
        sel = _rank_select(score, blk < own, blk, n_blocks, MOBA_TOP) | (blk == own)
        o_ref[0, r] = jnp.where(sel, 0.0, NEG_INF).astype(BF16)


def moba_select(proj, kmean, batch, tq=256):
    nb = kmean.shape[2]
    seq = proj.shape[0] // batch
    nq = seq // tq
    return pl.pallas_call(
        functools.partial(_moba_select_kernel, tq=tq, n_blocks=nb),
        grid=(batch, N_HEADS // HB, nq),
        in_specs=[pl.BlockSpec((tq, HB * HEAD_DIM), lambda b, g, i: (b * nq + i, g)),
                  pl.BlockSpec((1, HB, nb, HEAD_DIM), lambda b, g, i: (b, g, 0, 0))],
        out_specs=pl.BlockSpec((1, HB, nb, tq), lambda b, g, i: (b, g, 0, i)),
        out_shape=jax.ShapeDtypeStruct((batch, N_HEADS, nb, seq), BF16),
        compiler_params=_cparams("parallel", "parallel", "parallel"),
        name="moba_select",
    )(proj, kmean)


def _block_mean_kernel(k_ref, o_ref):
    o_ref[0] = jnp.mean(k_ref[...], axis=0, keepdims=True)


def block_mean(proj, col_blk, rows):
    t = proj.shape[0]
    return pl.pallas_call(
        _block_mean_kernel,
        grid=(t // rows,),
        in_specs=[pl.BlockSpec((rows, 1024), lambda i: (i, col_blk))],
        out_specs=pl.BlockSpec((1, 1, 1024), lambda i: (i, 0, 0)),
        out_shape=jax.ShapeDtypeStruct((t // rows, 1, 1024), F32),
        compiler_params=_cparams("parallel"),
        name="block_mean",
    )(proj)


def _nsa_cmp_kernel(slopes_ref, q_ref, kc_ref, vc_ref, o_ref, sel_ref, *, tq, n_blocks):
    g = pl.program_id(1)
    q0 = pl.program_id(2) * tq
    kc = kc_ref[0, 0]
    vc = vc_ref[0, 0]
    qpos_l = q0 + lax.broadcasted_iota(jnp.int32, (1, tq), 1)
    blk_s = lax.broadcasted_iota(jnp.int32, (n_blocks, 1), 0)
    own_l = qpos_l // CMP_BLOCK
    valid_t = blk_s < own_l
    dist_t = (qpos_l - (blk_s * CMP_BLOCK + (CMP_BLOCK - 1))).astype(F32)
    imp_t = jnp.zeros((n_blocks, tq), F32)
    for r in range(HB):
        slope = slopes_ref[g * HB + r]
        q = (q_ref[:, r * HEAD_DIM:(r + 1) * HEAD_DIM] * ATTN_SCALE).astype(BF16)
        s_t = jnp.where(valid_t, _nt(kc, q) - slope * dist_t, NEG_INF)
        e_t = jnp.where(valid_t, jnp.exp(s_t - jnp.max(s_t, axis=0, keepdims=True)), 0.0)
        p_t = e_t / jnp.maximum(jnp.sum(e_t, axis=0, keepdims=True), 1e-30)
        imp_t = imp_t + p_t
        o_ref[:, r * HEAD_DIM:(r + 1) * HEAD_DIM] = _tn(p_t.astype(BF16), vc)
    sel = _rank_select(imp_t, valid_t, blk_s, n_blocks, SEL_TOP) | (blk_s == own_l)
    sel_ref[0, 0] = jnp.where(sel, 0.0, NEG_INF).astype(BF16)


def nsa_cmp_prompt(proj, kc, vc, slopes, batch, tq=256):
    nb = kc.shape[2]
    seq = proj.shape[0] // batch
    nq = seq // tq
    return pl.pallas_call(
        functools.partial(_nsa_cmp_kernel, tq=tq, n_blocks=nb),
        grid_spec=pltpu.PrefetchScalarGridSpec(
            num_scalar_prefetch=1,
            grid=(batch, NSA_KV_HEADS, nq),
            in_specs=[pl.BlockSpec((tq, HB * HEAD_DIM), lambda b, g, i, s: (b * nq + i, g)),
                      pl.BlockSpec((1, 1, nb, HEAD_DIM), lambda b, g, i, s: (b, g, 0, 0)),
                      pl.BlockSpec((1, 1, nb, HEAD_DIM), lambda b, g, i, s: (b, g, 0, 0))],
            out_specs=[pl.BlockSpec((tq, HB * HEAD_DIM), lambda b, g, i, s: (b * nq + i, g)),
                       pl.BlockSpec((1, 1, nb, tq), lambda b, g, i, s: (b, g, 0, i))]),
        out_shape=[jax.ShapeDtypeStruct((batch * seq, N_HEADS * HEAD_DIM), F32),
                   jax.ShapeDtypeStruct((batch, NSA_KV_HEADS, nb, seq), BF16)],
        compiler_params=_cparams("parallel", "parallel", "parallel"),
        name="nsa_cmp_prompt",
    )(slopes, proj, kc, vc)


SEQ_PER_BATCH = 8
SEQ_PER_STEP = 4


def _compress_kernel(pt_ref, *refs, n_pages, kchunk):
    page_refs = refs[:n_pages]
    pe_ref, w1_ref, w2_ref, o_ref, stage_ref = refs[n_pages:]
    s = pl.program_id(0)
    slot = s % SEQ_PER_BATCH
    bpp = page_refs[0].shape[1]
    for p in range(n_pages):
        stage_ref[slot, p * bpp:(p + 1) * bpp, :] = page_refs[p][0]

    @pl.when(slot == SEQ_PER_BATCH - 1)
    def _():
        rows = SEQ_PER_BATCH * n_pages * bpp
        width = stage_ref.shape[-1]
        hid = jnp.zeros((rows, w1_ref.shape[1]), F32)
        for c in range(width // kchunk):
            x = stage_ref[:, :, c * kchunk:(c + 1) * kchunk].reshape(rows, kchunk)
            x = (x + pe_ref[:, c * kchunk:(c + 1) * kchunk]).astype(BF16)
            hid = hid + _nn(x, w1_ref[c * kchunk:(c + 1) * kchunk, :])
        act = (hid * _sigmoid(hid)).astype(BF16)
        o_ref[...] = _nn(act, w2_ref[...])


def compress_paged(pool2, page_table, pe_big, w1_big, w2_big):
    n_seq, n_pages = page_table.shape
    bpp, width = pool2.shape[1], pool2.shape[2]
    rows = SEQ_PER_BATCH * n_pages * bpp
    gw = w1_big.shape[1]
    in_specs = [pl.BlockSpec((1, bpp, width), functools.partial(lambda s, pt, p: (pt[s, p], 0, 0), p=p))
                for p in range(n_pages)]
    in_specs += [pl.BlockSpec((1, width), lambda s, pt: (0, 0)),
                 pl.BlockSpec((width, gw), lambda s, pt: (0, 0)),
                 pl.BlockSpec((gw, gw), lambda s, pt: (0, 0))]
    return pl.pallas_call(
        functools.partial(_compress_kernel, n_pages=n_pages, kchunk=2048),
        grid_spec=pltpu.PrefetchScalarGridSpec(
            num_scalar_prefetch=1,
            grid=(n_seq,),
            in_specs=in_specs,
            out_specs=pl.BlockSpec((rows, gw), lambda s, pt: (s // SEQ_PER_BATCH, 0)),
            scratch_shapes=[pltpu.VMEM((SEQ_PER_BATCH, n_pages * bpp, width), F32)]),
        out_shape=jax.ShapeDtypeStruct((n_seq * n_pages * bpp, gw), F32),
        compiler_params=_cparams("arbitrary"),
        name="compress_paged",
    )(page_table, *([pool2] * n_pages), pe_big, w1_big, w2_big)


COLS = 128


def _nsa_sample_cmp_kernel(qbd_ref, kc_ref, vc_ref, cs_ref, cp_ref, o_ref, mask_ref, *, n_blocks):
    qbd = qbd_ref[0]
    kc = kc_ref[0].astype(BF16)
    vc = vc_ref[0].astype(BF16)
    slope = cs_ref[0:1, :]
    qpos = cp_ref[0:1, :]
    blk = lax.broadcasted_iota(jnp.int32, (n_blocks, 1), 0)
    valid = blk < qpos // CMP_BLOCK
    dist = (qpos - (blk * CMP_BLOCK + (CMP_BLOCK - 1))).astype(F32)
    s = jnp.where(valid, _nn(kc, qbd) - slope * dist, NEG_INF)
    e = jnp.where(valid, jnp.exp(s - jnp.max(s, axis=0, keepdims=True)), 0.0)
    p = e / jnp.maximum(jnp.sum(e, axis=0, keepdims=True), 1e-30)
    o_ref[0] = _tn(p.astype(BF16), vc)
    tot = p + pltpu.roll(p, 4, 1) + pltpu.roll(p, 8, 1) + pltpu.roll(p, 12, 1)
    col = lax.broadcasted_iota(jnp.int32, (1, COLS), 1)
    tot = jnp.where(col % 16 >= 12, tot, 0.0)
    imp = tot + pltpu.roll(tot, COLS - 4, 1) + pltpu.roll(tot, COLS - 8, 1) + pltpu.roll(tot, COLS - 12, 1)
    sel = _rank_select(imp, valid, blk, n_blocks, SEL_TOP)
    mask_ref[0] = jnp.where(sel, 0.0, NEG_INF).astype(BF16)


def nsa_sample_cmp(qbd, kc, vc, col_slope, col_pos):
    n_seq, nb, c = kc.shape
    return pl.pallas_call(
        functools.partial(_nsa_sample_cmp_kernel, n_blocks=nb),
        grid=(n_seq,),
        in_specs=[pl.BlockSpec((1, c, COLS), lambda s: (s, 0, 0)),
                  pl.BlockSpec((1, nb, c), lambda s: (s, 0, 0)),
                  pl.BlockSpec((1, nb, c), lambda s: (s, 0, 0)),
                  pl.BlockSpec((8, COLS), lambda s: (0, 0)),
                  pl.BlockSpec((8, COLS), lambda s: (0, 0))],
        out_specs=[pl.BlockSpec((1, COLS, c), lambda s: (s, 0, 0)),
                   pl.BlockSpec((1, nb, COLS), lambda s: (s, 0, 0))],
        out_shape=[jax.ShapeDtypeStruct((n_seq, COLS, c), F32),
                   jax.ShapeDtypeStruct((n_seq, nb, COLS), BF16)],
        compiler_params=_cparams("parallel"),
        name="nsa_sample_cmp",
    )(qbd, kc, vc, col_slope, col_pos)


def _rank_select_lanes(v, cand, n_blocks, n_top):
    lane = lax.broadcasted_iota(jnp.int32, (1, v.shape[1]), 1)
    v = jnp.where(cand, v, -jnp.inf)
    rank = jnp.zeros(v.shape, jnp.int32)
    for i in range(n_blocks):
        vi = v[:, i:i + 1]
        ahead = (vi > v) | ((vi == v) & (lane > i))
        rank = rank + ahead.astype(jnp.int32)
    return cand & (rank < n_top)


def _sample_attn_kernel(pt_ref, q_ref, *refs, mode, n_chunks, kpos0, n_new):
    kt_refs = refs[:n_chunks]
    vt_refs = refs[n_chunks:2 * n_chunks]
    knew_ref, vnew_ref, rs_ref, rp_ref = refs[2 * n_chunks:2 * n_chunks + 4]
    pos = 2 * n_chunks + 4
    if mode == "sel":
        mask_ref, e_ref = refs[pos], refs[pos + 1]
        pos += 2
    o_ref, st_ref = refs[pos], refs[pos + 1]
    del pt_ref
    q = q_ref[0]
    slope = rs_ref[...]
    qpos = rp_ref[...]
    lane = lax.broadcasted_iota(jnp.int32, (1, PAGE_SIZE), 1)
    n_past = n_chunks * PAGE_SIZE

    if mode == "sel":
        sel_bias = _tn(mask_ref[0], e_ref[...])
    if mode == "moba":
        per_blk = MOBA_BLOCK // PAGE_SIZE
        n_blk = n_chunks // per_blk
        kmean = jnp.zeros((q.shape[1], PAGE_SIZE), F32)
        for j in range(n_blk):
            tot = kt_refs[j * per_blk][0]
            for i in range(1, per_blk):
                tot = tot + kt_refs[j * per_blk + i][0]
            mean_j = jnp.sum(tot, axis=1, keepdims=True) * (1.0 / MOBA_BLOCK)
            kmean = jnp.where(lane == j, mean_j, kmean)
        keep = _rank_select_lanes(_nn(q, kmean.astype(BF16)), lane < qpos // MOBA_BLOCK, n_blk, MOBA_TOP)
        moba_bias = jnp.where(keep, 0.0, NEG_INF)

    for p in range(n_chunks):
        kpos = kpos0 + p * PAGE_SIZE + lane
        s = _nn(q, kt_refs[p][0].astype(BF16)) - slope * (qpos - kpos).astype(F32)
        if mode == "sel":
            s = s + sel_bias[:, p * PAGE_SIZE:(p + 1) * PAGE_SIZE]
        elif mode == "moba":
            s = s + moba_bias[:, p // per_blk:p // per_blk + 1]
        else:
            diff = qpos - kpos
            s = jnp.where((kpos >= 0) & (diff >= 0) & (diff < WINDOW), s, NEG_INF)
        st_ref[:, p * PAGE_SIZE:(p + 1) * PAGE_SIZE] = s
    kpos_new = kpos0 + n_past + lane
    s_new = _nn(q, knew_ref[0].astype(BF16)) - slope * (qpos - kpos_new).astype(F32)
    st_ref[:, n_past:] = jnp.where((lane < n_new) & (kpos_new <= qpos), s_new, NEG_INF)

    s_all = st_ref[...]
    e = jnp.exp(s_all - jnp.max(s_all, axis=1, keepdims=True))
    p_all = (e / jnp.sum(e, axis=1, keepdims=True)).astype(BF16)
    o = _nt(p_all[:, n_past:], vnew_ref[0].astype(BF16))
    for p in range(n_chunks):
        o = o + _nt(p_all[:, p * PAGE_SIZE:(p + 1) * PAGE_SIZE], vt_refs[p][0].astype(BF16))
    o_ref[0] = o


def sample_attn(mode, q_rows, kt_src, vt_src, page_table, kt_new, vt_new, row_slope, row_pos, kpos0, n_new,
                page0=0, win_chunks=0, mask=None, block_onehot=None):
    n_seq, rows, c = q_rows.shape
    if mode == "win":
        n_chunks = win_chunks
        src_spec = [pl.BlockSpec((1, c, PAGE_SIZE), functools.partial(lambda s, pt, p: (page0 + s, 0, p), p=p))
                    for p in range(n_chunks)]
    else:
        n_chunks = page_table.shape[1]
        src_spec = [pl.BlockSpec((1, c, PAGE_SIZE), functools.partial(
            lambda s, pt, p: (page0 + pt[s, p], 0, 0), p=p)) for p in range(n_chunks)]
    in_specs = [pl.BlockSpec((1, rows, c), lambda s, pt: (s, 0, 0))] + src_spec + src_spec
    in_specs += [pl.BlockSpec((1, c, PAGE_SIZE), lambda s, pt: (s, 0, 0)),
                 pl.BlockSpec((1, c, PAGE_SIZE), lambda s, pt: (s, 0, 0)),
                 pl.BlockSpec((rows, PAGE_SIZE), lambda s, pt: (0, 0)),
                 pl.BlockSpec((rows, PAGE_SIZE), lambda s, pt: (0, 0))]
    args = [q_rows] + [kt_src] * n_chunks + [vt_src] * n_chunks + [kt_new, vt_new, row_slope, row_pos]
    if mode == "sel":
        nb = mask.shape[1]
        in_specs += [pl.BlockSpec((1, nb, rows), lambda s, pt: (s, 0, 0)),
                     pl.BlockSpec(block_onehot.shape, lambda s, pt: (0, 0))]
        args += [mask, block_onehot]
    kern = functools.partial(_sample_attn_kernel, mode=mode, n_chunks=n_chunks, kpos0=kpos0, n_new=n_new)
    return pl.pallas_call(
        kern,
        grid_spec=pltpu.PrefetchScalarGridSpec(
            num_scalar_prefetch=1,
            grid=(n_seq,),
            in_specs=in_specs,
            out_specs=pl.BlockSpec((1, rows, c), lambda s, pt: (s, 0, 0)),
            scratch_shapes=[pltpu.VMEM((rows, (n_chunks + 1) * PAGE_SIZE), F32)]),
        out_shape=jax.ShapeDtypeStruct((n_seq, rows, c), F32),
        compiler_params=_cparams("parallel"),
        name="sample_attn_" + mode,
    )(page_table, *args)


def _compress_native_kernel(pt_ref, *refs, n_pages, n_groups):
    n_in = SEQ_PER_STEP * n_pages
    page_refs = refs[:n_in]
    pe_ref, w1_ref, w2_ref, o_ref, stage_ref = refs[n_in:]
    del pt_ref
    steps_per_batch = SEQ_PER_BATCH // SEQ_PER_STEP
    slot = pl.program_id(0) % steps_per_batch
    page_rows = n_groups * HEAD_DIM
    for ip in range(n_in):
        start = pl.multiple_of((slot * n_in + ip) * page_rows, page_rows)
        stage_ref[pl.ds(start, page_rows), :] = page_refs[ip][0]

    @pl.when(slot == steps_per_batch - 1)
    def _():
        n_rows = SEQ_PER_BATCH * n_pages * n_groups
        hid = jnp.zeros((n_rows, PAGE_SIZE), F32)
        for dd in range(HEAD_DIM // 2):
            x0 = stage_ref[pl.ds(2 * dd, n_rows, stride=HEAD_DIM), :] + pe_ref[2 * dd:2 * dd + 1, :]
            x1 = stage_ref[pl.ds(2 * dd + 1, n_rows, stride=HEAD_DIM), :] + pe_ref[2 * dd + 1:2 * dd + 2, :]
            hid = hid + _nn(jnp.concatenate([x0, x1], axis=1).astype(BF16), w1_ref[dd])
        act = (hid * _sigmoid(hid)).astype(BF16)
        o_ref[...] = _nn(act, w2_ref[...])


def compress_native(pool_t, page_table, pe_t, w1_pairs, w2_blocks):
    n_seq, n_pages = page_table.shape
    page_rows = pool_t.shape[1]
    n_groups = page_rows // HEAD_DIM
    out_rows = SEQ_PER_BATCH * n_pages * n_groups
    in_specs = [pl.BlockSpec((1, page_rows, PAGE_SIZE), functools.partial(
        lambda s, pt, i, p: (pt[s * SEQ_PER_STEP + i, p], 0, 0), i=i, p=p))
        for i in range(SEQ_PER_STEP) for p in range(n_pages)]
    in_specs += [pl.BlockSpec(pe_t.shape, lambda s, pt: (0, 0)),
                 pl.BlockSpec(w1_pairs.shape, lambda s, pt: (0, 0, 0)),
                 pl.BlockSpec(w2_blocks.shape, lambda s, pt: (0, 0))]
    steps_per_batch = SEQ_PER_BATCH // SEQ_PER_STEP
    return pl.pallas_call(
        functools.partial(_compress_native_kernel, n_pages=n_pages, n_groups=n_groups),
        grid_spec=pltpu.PrefetchScalarGridSpec(
            num_scalar_prefetch=1,
            grid=(n_seq // SEQ_PER_STEP,),
            in_specs=in_specs,
            out_specs=pl.BlockSpec((out_rows, PAGE_SIZE), lambda s, pt: (s // steps_per_batch, 0)),
            scratch_shapes=[pltpu.VMEM((SEQ_PER_BATCH * n_pages * page_rows, PAGE_SIZE), F32)]),
        out_shape=jax.ShapeDtypeStruct((n_seq * n_pages * n_groups, PAGE_SIZE), F32),
        compiler_params=_cparams("arbitrary"),
        name="compress_native",
    )(page_table, *([pool_t] * (SEQ_PER_STEP * n_pages)), pe_t, w1_pairs, w2_blocks)


NSA_KV_W = NSA_KV_HEADS * HEAD_DIM
NSA_PROJ_W = 3840
NSA_Z_BLK, NSA_KV_BLK0, NSA_GL_BLK = 1, 8, 14


def _heads_major(t2d, batch, n_heads):
    n = t2d.shape[0] // batch
    return t2d.reshape(batch, n, n_heads, HEAD_DIM).transpose(0, 2, 1, 3).astype(BF16)


def _heads_major_v(t2d, batch, n_heads):
    v = _heads_major(t2d, batch, n_heads)
    ones = jnp.ones(v.shape[:-1] + (1,), BF16)
    return jnp.concatenate([v, ones, jnp.zeros(v.shape[:-1] + (LANES - HEAD_DIM - 1,), BF16)], axis=-1)


def _block_onehot(seq, block):
    return (jnp.arange(seq)[:, None] // block == jnp.arange(seq // block)[None, :]).astype(BF16)


def _split3(x):
    hi = x.astype(BF16)
    r1 = x - hi.astype(F32)
    mid = r1.astype(BF16)
    return hi, mid, (r1 - mid.astype(F32)).astype(BF16)


def _pad_lanes(parts, width):
    used = sum(p.shape[-1] for p in parts)
    return parts + [jnp.zeros(parts[0].shape[:-1] + (width - used,), BF16)]


def _aug_queries(q2d, slopes, mask=None):
    t = q2d.shape[0]
    qs = (q2d * (ATTN_SCALE * LOG2E)).reshape(t, N_HEADS, HEAD_DIM).astype(BF16)
    hi, mid, lo = _split3(slopes * LOG2E)
    pieces = jnp.stack([hi, hi, mid, mid, lo, lo], axis=-1)
    parts = _pad_lanes([qs, jnp.broadcast_to(pieces[None], (t, N_HEADS, ALIBI_COLS))], LANES)
    if mask is not None:
        parts = parts + _pad_lanes([mask], LANES)
    return jnp.concatenate(parts, axis=-1).reshape(t, -1)


def _aug_keys(k2d, batch, n_heads, block=0):
    k = _heads_major(k2d, batch, n_heads)
    n = k.shape[2]
    pos = jnp.arange(n)
    pos_hi, pos_lo = (pos // 64 * 64).astype(BF16), (pos % 64).astype(BF16)
    pcols = jnp.stack([pos_hi, pos_lo] * (ALIBI_COLS // 2), axis=-1)
    lead = k.shape[:2]
    parts = _pad_lanes([k, jnp.broadcast_to(pcols[None, None], lead + (n, ALIBI_COLS))], LANES)
    if block:
        parts = parts + _pad_lanes([jnp.broadcast_to(_block_onehot(n, block)[None, None],
                                                     lead + (n, n // block))], LANES)
    return jnp.concatenate(parts, axis=-1)


def _column_tables(slopes, past, n_tok):
    col = jnp.arange(COLS)
    live = col < N_HEADS * n_tok
    head = jnp.minimum(col // n_tok, N_HEADS - 1)
    slope = jnp.where(live, slopes[head], 0.0)
    pos = jnp.where(live, past + col % n_tok, past).astype(jnp.int32)
    grp = jnp.where(live, head // NSA_GROUP, -1).astype(jnp.int32)
    ints = jnp.concatenate([pos[None, :], grp[None, :], jnp.zeros((6, COLS), jnp.int32)], axis=0)
    return jnp.broadcast_to(slope[None, :], (8, COLS)), ints


def _row_tables(col_slope, col_ints):
    return (jnp.broadcast_to(col_slope[0][:, None], (COLS, PAGE_SIZE)),
            jnp.broadcast_to(col_ints[0][:, None], (COLS, PAGE_SIZE)))


def _new_rows_t(t2d, n_seq, n_tok):
    rows_t = t2d.reshape(n_seq, n_tok, -1).transpose(0, 2, 1)
    return jnp.pad(rows_t, ((0, 0), (0, 0), (0, PAGE_SIZE - n_tok)))


def _stored_tiles(cache):
    nd = cache.ndim
    t = jnp.transpose(cache, tuple(range(nd - 3)) + (nd - 2, nd - 1, nd - 3))
    return t.reshape((-1, cache.shape[-2] * cache.shape[-1], cache.shape[-3]))


def _expand_query(q2d, n_seq, n_tok, heads_per_key):
    q4 = (q2d * ATTN_SCALE).reshape(n_seq, n_tok, N_HEADS, HEAD_DIM)
    n_keys = N_HEADS // heads_per_key
    owner = (jnp.arange(N_HEADS)[:, None] // heads_per_key == jnp.arange(n_keys)[None, :])
    qbd = jnp.where(owner[None, None, :, None, :], q4[..., None], 0.0)
    qbd = qbd.transpose(0, 4, 3, 2, 1).reshape(n_seq, n_keys * HEAD_DIM, N_HEADS * n_tok)
    return jnp.pad(qbd, ((0, 0), (0, 0), (0, COLS - N_HEADS * n_tok))).astype(BF16)


def _take_own(o_t, n_seq, n_tok, heads_per_key):
    n_keys = N_HEADS // heads_per_key
    o5 = o_t[:, :N_HEADS * n_tok].reshape(n_seq, N_HEADS, n_tok, n_keys, HEAD_DIM)
    hh = jnp.arange(N_HEADS)
    own = o5[:, hh, :, hh // heads_per_key, :]
    return own.transpose(1, 2, 0, 3).reshape(n_seq * n_tok, N_HEADS * HEAD_DIM)


def _pad_rows(t2d, n_seq, n_tok):
    return jnp.pad(t2d.reshape(n_seq, n_tok, -1), ((0, 0), (0, 8 - n_tok), (0, 0)))


def _nsa_weights(w_in, pe_k, pe_v, w1_k, w2_k, w1_v, w2_v):
    d = w_in.shape[0]
    q, kv, gl, z = (w_in[:, :1024], w_in[:, 1024:1024 + 6 * NSA_KV_W],
                    w_in[:, 1024 + 6 * NSA_KV_W:1024 + 6 * NSA_KV_W + 3 * N_HEADS], w_in[:, -1024:])
    pad = jnp.zeros((d, NSA_PROJ_W - (2048 + 6 * NSA_KV_W + 3 * N_HEADS)), w_in.dtype)
    w_re = jnp.concatenate([q, z, kv, gl, pad], axis=1).astype(BF16)
    eye = jnp.eye(NSA_KV_HEADS, dtype=w1_k.dtype)

    def big(pe, w1, w2):
        hid = w1.shape[1]
        w1r = w1.reshape(CMP_BLOCK, HEAD_DIM, hid)
        w1b = (w1r[:, None, :, None, :] * eye[None, :, None, :, None]).reshape(
            CMP_BLOCK * NSA_KV_W, NSA_KV_HEADS * hid).astype(BF16)
        w2b = (w2[None, :, None, :] * eye[:, None, :, None]).reshape(
            NSA_KV_HEADS * hid, NSA_KV_W).astype(BF16)
        peb = jnp.broadcast_to(pe[:, None, :], (CMP_BLOCK, NSA_KV_HEADS, HEAD_DIM)).reshape(1, -1)
        return peb, w1b, w2b

    bpp = PAGE_SIZE // CMP_BLOCK
    eye_b = jnp.eye(bpp, dtype=w1_k.dtype)

    def stored(pe, w1, w2):
        hid = w1.shape[1]
        w1r = w1.reshape(CMP_BLOCK, HEAD_DIM, hid).transpose(1, 0, 2)
        w1d = (w1r[:, None, :, None, :] * eye_b[None, :, None, :, None]).reshape(
            HEAD_DIM, bpp * CMP_BLOCK, bpp * hid)
        w1p = w1d.reshape(HEAD_DIM // 2, 2 * bpp * CMP_BLOCK, bpp * hid).astype(BF16)
        w2b = (w2[None, :, None, :] * eye_b[:, None, :, None]).reshape(bpp * hid, bpp * HEAD_DIM).astype(BF16)
        return jnp.tile(pe.T, (1, bpp)), w1p, w2b

    return (w_re, big(pe_k, w1_k, w2_k), big(pe_v, w1_v, w2_v), stored(pe_k, w1_k, w2_k),
            stored(pe_v, w1_v, w2_v))


def _nsa_prompt(x, nw, w_re, cmp_k, cmp_v, w_out, slopes, batch, final_w):
    t = x.shape[0]
    n = t // batch
    proj = norm_matmul(x, nw, w_re)
    kv = [proj[:, 2048 + NSA_KV_W * i:2048 + NSA_KV_W * (i + 1)] for i in range(6)]
    ck, cv, sk, sv, wk, wv = kv
    row_w = CMP_BLOCK * NSA_KV_W
    bpp = PAGE_SIZE // CMP_BLOCK
    n_pages = t // PAGE_SIZE
    pt = jnp.arange(n_pages, dtype=jnp.int32).reshape(SEQ_PER_BATCH, n_pages // SEQ_PER_BATCH)
    nb = n // CMP_BLOCK
    comp = lambda rows, wts: compress_paged(rows.reshape(n_pages, bpp, row_w), pt, *wts).reshape(
        batch, nb, NSA_KV_HEADS, HEAD_DIM).transpose(0, 2, 1, 3).astype(BF16)
    o_cmp, sel_t = nsa_cmp_prompt(proj, comp(ck, cmp_k), comp(cv, cmp_v), slopes, batch)
    mask = jnp.repeat(sel_t.transpose(0, 3, 1, 2), NSA_GROUP, axis=2).reshape(t, N_HEADS, nb)
    o_sel = flash_prompt(_aug_queries(proj[:, :1024], slopes, mask), _aug_keys(sk, batch, NSA_KV_HEADS, CMP_BLOCK),
                         _heads_major_v(sv, batch, NSA_KV_HEADS), batch, tq=128, tk=1024)
    o_win = flash_prompt(_aug_queries(proj[:, :1024], slopes), _aug_keys(wk, batch, NSA_KV_HEADS),
                         _heads_major_v(wv, batch, NSA_KV_HEADS), batch, window=WINDOW, tq=256, tk=256)
    y = out_proj(x, proj, NSA_Z_BLK, [o_cmp, o_sel, o_win], w_out, gl_blk=NSA_GL_BLK, final_w=final_w)
    st = lambda a: a.reshape(batch, n, NSA_KV_HEADS, HEAD_DIM)
    keep = min(WINDOW, n)
    return y, (st(ck), st(cv), st(sk), st(sv), st(wk)[:, n - keep:], st(wv)[:, n - keep:])


def _nsa_sample(x, nw, w_re, cmp_k, cmp_v, w_out, slopes, n_seq, final_w, page_table, layer_j, caches):
    ck_t, cv_t, sk_t, sv_t, wk_t, wv_t, win_k, win_v, n_pool = caches
    n_tok = x.shape[0] // n_seq
    n_pages = page_table.shape[1]
    past = n_pages * PAGE_SIZE
    win_len = win_k.shape[2]
    bpp = PAGE_SIZE // CMP_BLOCK
    proj = norm_matmul(x, nw, w_re)
    kv = [proj[:, 2048 + NSA_KV_W * i:2048 + NSA_KV_W * (i + 1)] for i in range(6)]
    ck, cv, sk, sv, wk, wv = kv
    pt_layer = page_table + layer_j * n_pool
    comp = lambda pool_t, wts: compress_native(pool_t, pt_layer, *wts).reshape(
        n_seq, n_pages, NSA_KV_HEADS, bpp, HEAD_DIM).transpose(0, 1, 3, 2, 4).reshape(
        n_seq, n_pages * bpp, NSA_KV_W)
    col_slope, col_ints = _column_tables(slopes, past, n_tok)
    qbd = _expand_query(proj[:, :1024], n_seq, n_tok, NSA_GROUP)
    o_cmp, mask = nsa_sample_cmp(qbd, comp(ck_t, cmp_k), comp(cv_t, cmp_v), col_slope, col_ints)
    q_rows = qbd.transpose(0, 2, 1)
    row_slope, row_pos = _row_tables(col_slope, col_ints)
    new_t = lambda a: _new_rows_t(a, n_seq, n_tok)
    block_onehot = (jnp.arange(past // CMP_BLOCK)[:, None] == jnp.arange(past)[None, :] // CMP_BLOCK).astype(BF16)
    o_sel = sample_attn("sel", q_rows, sk_t, sv_t, page_table, new_t(sk), new_t(sv), row_slope, row_pos, 0, n_tok,
                        page0=layer_j * n_pool, mask=mask, block_onehot=block_onehot)
    o_win = sample_attn("win", q_rows, wk_t, wv_t, page_table, new_t(wk), new_t(wv), row_slope, row_pos,
                        past - win_len, n_tok, page0=layer_j * n_seq, win_chunks=win_len // PAGE_SIZE)
    own = lambda o: _take_own(o, n_seq, n_tok, NSA_GROUP)
    y = out_proj(x, proj, NSA_Z_BLK, [own(o_cmp), own(o_sel), own(o_win)], w_out, gl_blk=NSA_GL_BLK,
                 final_w=final_w)
    st = lambda a: a.reshape(n_seq, n_tok, NSA_KV_HEADS, HEAD_DIM)
    kw = jnp.concatenate([win_k[layer_j], st(wk)], axis=1)
    vw = jnp.concatenate([win_v[layer_j], st(wv)], axis=1)
    keep = min(WINDOW, kw.shape[1])
    return y, (st(ck), st(cv), st(sk), st(sv), kw[:, kw.shape[1] - keep:], vw[:, vw.shape[1] - keep:])


def _nsa_cache_views(cmp_k, cmp_v, sel_k, sel_v, win_k, win_v):
    n_pool = cmp_k.shape[1]
    return (_stored_tiles(cmp_k), _stored_tiles(cmp_v), _stored_tiles(sel_k), _stored_tiles(sel_v),
            _stored_tiles(win_k), _stored_tiles(win_v), win_k, win_v, n_pool)


def _moba_prompt(x, nw, w_in, w_out, slopes, batch, final_w):
    t = x.shape[0]
    n = t // batch
    proj = norm_matmul(x, nw, w_in)
    k, v = proj[:, 1024:2048], proj[:, 2048:3072]
    nf = n // MOBA_BLOCK
    kmean = block_mean(proj, 1, MOBA_BLOCK).reshape(batch, nf, N_HEADS, HEAD_DIM).transpose(0, 2, 1, 3).astype(BF16)
    sel_t = moba_select(proj, kmean, batch)
    mask = sel_t.transpose(0, 3, 1, 2).reshape(t, N_HEADS, nf)
    o = flash_prompt(_aug_queries(proj[:, :1024], slopes, mask), _aug_keys(k, batch, N_HEADS, MOBA_BLOCK),
                     _heads_major_v(v, batch, N_HEADS), batch, tq=512, tk=1024)
    y = out_proj(x, proj, 3, [o], w_out, final_w=final_w)
    st = lambda a: a.reshape(batch, n, N_HEADS, HEAD_DIM)
    return y, (st(k), st(v))


def _moba_sample(x, nw, w_in, w_out, slopes, n_seq, final_w, page_table, k_pool, v_pool):
    n_tok = x.shape[0] // n_seq
    past = page_table.shape[1] * PAGE_SIZE
    assert past % MOBA_BLOCK == 0 and n_tok <= MOBA_BLOCK
    proj = norm_matmul(x, nw, w_in)
    k, v = proj[:, 1024:2048], proj[:, 2048:3072]
    q_rows = _expand_query(proj[:, :1024], n_seq, n_tok, 1).transpose(0, 2, 1)
    row_slope, row_pos = _row_tables(*_column_tables(slopes, past, n_tok))
    o = sample_attn("moba", q_rows, _stored_tiles(k_pool), _stored_tiles(v_pool), page_table,
                    _new_rows_t(k, n_seq, n_tok), _new_rows_t(v, n_seq, n_tok), row_slope, row_pos, 0, n_tok)
    y = out_proj(x, proj, 3, [_take_own(o, n_seq, n_tok, 1)], w_out, final_w=final_w)
    st = lambda a: a.reshape(n_seq, n_tok, N_HEADS, HEAD_DIM)
    return y, (st(k), st(v))


def _ret_prompt_layer(x, nw, w_in, gn_w, gn_b, w_out, batch, final_w):
    proj = norm_matmul(x, nw, w_in)
    on, state = ret_prompt(proj, batch, gn_w, gn_b)
    return out_proj(x, proj, 3, [on], w_out, final_w=final_w), state


def _ret_sample_layer(x, nw, w_in, gn_w, gn_b, w_out, n_seq, final_w, state):
    n_tok = x.shape[0] // n_seq
    proj = norm_matmul(x, nw, w_in)
    on, new_state = ret_sample(_pad_rows(proj, n_seq, n_tok), state, gn_w, gn_b, n_tok)
    on = on[:, :n_tok].reshape(n_seq * n_tok, -1)
    return out_proj(x, proj, 3, [on], w_out, final_w=final_w), new_state


def kernel(x_prompt, x_sample, cache_nsa_cmp_k, cache_nsa_cmp_v, cache_nsa_sel_k, cache_nsa_sel_v,
           cache_nsa_win_k, cache_nsa_win_v, cache_moba_k, cache_moba_v, state_ret, page_table,
           norm_w, final_norm_w, nsa_w_in, nsa_pe_k, nsa_pe_v, nsa_w1_k, nsa_w2_k, nsa_w1_v, nsa_w2_v,
           nsa_w_out, moba_w_in, moba_w_out, ret_w_in, ret_gn_w, ret_gn_b, ret_w_out):
    batch, seq, d = x_prompt.shape
    n_seq, n_tok, _ = x_sample.shape
    depth = norm_w.shape[0]
    slopes = jnp.exp2(-8.0 * (jnp.arange(N_HEADS, dtype=F32) + 1.0) / N_HEADS)
    xp = x_prompt.reshape(batch * seq, d)
    xs = x_sample.reshape(n_seq * n_tok, d)
    nsa_p, nsa_s, moba_p, moba_s, ret_p, ret_s = [], [], [], [], [], []
    nsa_caches = _nsa_cache_views(cache_nsa_cmp_k, cache_nsa_cmp_v, cache_nsa_sel_k, cache_nsa_sel_v,
                                  cache_nsa_win_k, cache_nsa_win_v)
    for layer in range(depth):
        j = layer // N_MIXERS
        fw = final_norm_w if layer == depth - 1 else None
        nw = norm_w[layer]
        if layer % N_MIXERS == 0:
            w_re, cmp_k, cmp_v, cmp_k_st, cmp_v_st = _nsa_weights(
                nsa_w_in[j], nsa_pe_k[j], nsa_pe_v[j], nsa_w1_k[j], nsa_w2_k[j], nsa_w1_v[j], nsa_w2_v[j])
            w_out = nsa_w_out[j].astype(BF16)
            xp, stp = _nsa_prompt(xp, nw, w_re, cmp_k, cmp_v, w_out, slopes, batch, fw)
            xs, sts = _nsa_sample(xs, nw, w_re, cmp_k_st, cmp_v_st, w_out, slopes, n_seq, fw, page_table, j,
                                  nsa_caches)
            nsa_p.append(stp)
            nsa_s.append(sts)
        elif layer % N_MIXERS == 1:
            w_in, w_out = moba_w_in[j].astype(BF16), moba_w_out[j].astype(BF16)
            xp, stp = _moba_prompt(xp, nw, w_in, w_out, slopes, batch, fw)
            xs, sts = _moba_sample(xs, nw, w_in, w_out, slopes, n_seq, fw, page_table, cache_moba_k[j],
                                   cache_moba_v[j])
            moba_p.append(stp)
            moba_s.append(sts)
        else:
            w_in, w_out = ret_w_in[j].astype(BF16), ret_w_out[j].astype(BF16)
            xp, stp = _ret_prompt_layer(xp, nw, w_in, ret_gn_w[j], ret_gn_b[j], w_out, batch, fw)
            xs, sts = _ret_sample_layer(xs, nw, w_in, ret_gn_w[j], ret_gn_b[j], w_out, n_seq, fw, state_ret[j])
            ret_p.append(stp)
            ret_s.append(sts)
    st = lambda items, i: jnp.stack([s[i] for s in items])
    return (xp.reshape(batch, seq, d), xs.reshape(n_seq, n_tok, d),
            st(nsa_p, 0), st(nsa_p, 1), st(nsa_p, 2), st(nsa_p, 3), st(nsa_p, 4), st(nsa_p, 5),
            st(moba_p, 0), st(moba_p, 1), jnp.stack(ret_p),
            st(nsa_s, 0), st(nsa_s, 1), st(nsa_s, 2), st(nsa_s, 3), st(nsa_s, 4), st(nsa_s, 5),
            st(moba_s, 0), st(moba_s, 1), jnp.stack(ret_s))
```

```python
import functools

import jax
import jax.numpy as jnp
import numpy as np
from jax import lax
from jax.experimental import pallas as pl
from jax.experimental.pallas import tpu as pltpu

F32 = jnp.float32
BF16 = jnp.bfloat16

HEAD_DIM = 64
N_HEADS = 16
NSA_KV_HEADS = 4
NSA_GROUP = 4
CMP_BLOCK = 64
SEL_TOP = 15
WINDOW = 512
MOBA_BLOCK = 256
MOBA_TOP = 3
RET_HEADS = 4
RET_DK = 256
RET_CHUNK = 128
PAGE_SIZE = 128
RMS_EPS = 1e-6
GN_EPS = 1e-5
NEG_INF = -1e30
ATTN_SCALE = HEAD_DIM ** -0.5
N_MIXERS = 3

VMEM_LIMIT = 56 * 1024 * 1024


def _cparams(*sem):
    return pltpu.CompilerParams(dimension_semantics=sem, vmem_limit_bytes=VMEM_LIMIT)


def _nt(a, b):
    return lax.dot_general(a, b, (((1,), (1,)), ((), ())), preferred_element_type=F32)


def _nn(a, b):
    return lax.dot_general(a, b, (((1,), (0,)), ((), ())), preferred_element_type=F32)


def _tn(a, b):
    return lax.dot_general(a, b, (((0,), (0,)), ((), ())), preferred_element_type=F32)


def _sigmoid(x):
    return 1.0 / (1.0 + jnp.exp(-x))


def _norm_matmul_kernel(x_ref, nw_ref, w_ref, o_ref, xn_ref):
    @pl.when(pl.program_id(1) == 0)
    def _():
        x = x_ref[...]
        ms = jnp.mean(x * x, axis=-1, keepdims=True)
        xn_ref[...] = ((x * lax.rsqrt(ms + RMS_EPS)) * nw_ref[...]).astype(BF16)

    o_ref[...] = _nn(xn_ref[...], w_ref[...])


def norm_matmul(x, nw, w_bf16):
    t, d = x.shape
    n = w_bf16.shape[1]
    tm = min(t, 1024)
    tn = 1024 if n % 1024 == 0 else 768
    assert n % tn == 0
    return pl.pallas_call(
        _norm_matmul_kernel,
        grid=(t // tm, n // tn),
        in_specs=[pl.BlockSpec((tm, d), lambda i, j: (i, 0)),
                  pl.BlockSpec((1, d), lambda i, j: (0, 0)),
                  pl.BlockSpec((d, tn), lambda i, j: (0, j))],
        out_specs=pl.BlockSpec((tm, tn), lambda i, j: (i, j)),
        out_shape=jax.ShapeDtypeStruct((t, n), F32),
        scratch_shapes=[pltpu.VMEM((tm, d), BF16)],
        compiler_params=_cparams("parallel", "arbitrary"),
        name="norm_matmul",
    )(x, nw.reshape(1, d), w_bf16)


def _out_proj_kernel(*refs, n_o, nsa_gates, final_norm):
    x_ref, z_ref = refs[0], refs[1]
    o_refs = refs[2:2 + n_o]
    pos = 2 + n_o
    if nsa_gates:
        gl_ref = refs[pos]
        pos += 1
    w_ref = refs[pos]
    pos += 1
    if final_norm:
        fw_ref = refs[pos]
        pos += 1
    y_ref = refs[pos]

    if nsa_gates:
        gates = _sigmoid(gl_ref[...])
        tm = gates.shape[0]
        lane = lax.broadcasted_iota(jnp.int32, (tm, 128), 1)
        cols = []
        for vb in range(N_HEADS // 2):
            acc = None
            for br in range(n_o):
                c0 = (2 * vb) * 3 + br
                c1 = (2 * vb + 1) * 3 + br
                g = jnp.where(lane < HEAD_DIM,
                              jnp.broadcast_to(gates[:, c0:c0 + 1], (tm, 128)),
                              jnp.broadcast_to(gates[:, c1:c1 + 1], (tm, 128)))
                term = g * o_refs[br][:, vb * 128:(vb + 1) * 128]
                acc = term if acc is None else acc + term
            cols.append(acc)
        o = jnp.concatenate(cols, axis=1)
    else:
        o = o_refs[0][...]
    z = z_ref[...]
    gated = (o * (z * _sigmoid(z))).astype(BF16)
    y = x_ref[...] + _nn(gated, w_ref[...])
    if final_norm:
        ms = jnp.mean(y * y, axis=-1, keepdims=True)
        y = (y * lax.rsqrt(ms + RMS_EPS)) * fw_ref[...]
    y_ref[...] = y


def out_proj(x, proj, z_blk, o_list, w_bf16, gl_blk=None, final_w=None, tm=256):
    t, d = x.shape
    tm = min(tm, t)
    n_o = len(o_list)
    in_specs = [pl.BlockSpec((tm, d), lambda i: (i, 0)),
                pl.BlockSpec((tm, d), lambda i: (i, z_blk))]
    args = [x, proj]
    for o in o_list:
        in_specs.append(pl.BlockSpec((tm, d), lambda i: (i, 0)))
        args.append(o)
    if gl_blk is not None:
        in_specs.append(pl.BlockSpec((tm, 256), lambda i: (i, gl_blk)))
        args.append(proj)
    in_specs.append(pl.BlockSpec((d, d), lambda i: (0, 0)))
    args.append(w_bf16)
    if final_w is not None:
        in_specs.append(pl.BlockSpec((1, d), lambda i: (0, 0)))
        args.append(final_w.reshape(1, d))
    kern = functools.partial(_out_proj_kernel, n_o=n_o, nsa_gates=gl_blk is not None,
                             final_norm=final_w is not None)
    return pl.pallas_call(
        kern,
        grid=(t // tm,),
        in_specs=in_specs,
        out_specs=pl.BlockSpec((tm, d), lambda i: (i, 0)),
        out_shape=jax.ShapeDtypeStruct((t, d), F32),
        compiler_params=_cparams("parallel"),
        name="out_proj",
    )(*args)


def _ret_tables(c_true, c_pad):
    lg = jnp.log1p(-jnp.exp2(-5.0 - jnp.arange(RET_HEADS, dtype=F32)))[:, None, None]
    i = jnp.arange(c_pad, dtype=F32)
    live = (i < c_true)
    diff = i[:, None] - i[None, :]
    dmat = jnp.where((diff >= 0) & live[:, None] & live[None, :], jnp.exp(lg * jnp.maximum(diff, 0.0)), 0.0)
    qdec = jnp.exp(lg * (i[:, None] + 1.0)) * jnp.ones((1, 1, 128), F32)
    kdec = jnp.where(live[:, None], jnp.exp(lg * (c_true - 1.0 - i[:, None])), 0.0) * jnp.ones((1, 1, 128), F32)
    sdec = jnp.exp(lg * c_true) * jnp.ones((1, 8, 128), F32)
    return dmat, qdec, kdec, sdec


def _ret_chunk(q_ref, k_ref, v_ref, d_ref, qd_ref, kd_ref, sd_ref, gw_ref, gb_ref, o_ref, st_ref):
    shp = q_ref.shape
    q = q_ref[...].reshape(shp[-2], shp[-1])
    k = k_ref[...].reshape(shp[-2], shp[-1]) * (RET_DK ** -0.5)
    v = v_ref[...].reshape(shp[-2], shp[-1])
    qb, kb, vb = q.astype(BF16), k.astype(BF16), v.astype(BF16)
    state = st_ref[...]
    dmat = d_ref[0]
    qdec = qd_ref[0][:, 0:1]
    kdec = kd_ref[0][:, 0:1]
    sdec = sd_ref[0][0:1, 0:1]
    inner = _nt(qb, kb) * dmat
    o = _nn(inner.astype(BF16), vb) + _nn(qb, state.astype(BF16)) * qdec
    kw = (k * kdec).astype(BF16)
    new_state = sdec * state + _nn(kw.T, vb)

    mu = jnp.mean(o, axis=-1, keepdims=True)
    var = jnp.mean(jnp.square(o - mu), axis=-1, keepdims=True)
    on = (o - mu) * lax.rsqrt(var + GN_EPS)
    o_ref[...] = (on * gw_ref[...] + gb_ref[...]).reshape(o_ref.shape)
    return new_state


def _ret_prompt_kernel(q_ref, k_ref, v_ref, d_ref, qd_ref, kd_ref, sd_ref, gw_ref, gb_ref,
                       o_ref, sn_ref, st_ref, *, n_chunks):
    c = pl.program_id(2)

    @pl.when(c == 0)
    def _():
        st_ref[...] = jnp.zeros_like(st_ref)

    new_state = _ret_chunk(q_ref, k_ref, v_ref, d_ref, qd_ref, kd_ref, sd_ref, gw_ref, gb_ref, o_ref, st_ref)
    st_ref[...] = new_state

    @pl.when(c == n_chunks - 1)
    def _():
        sn_ref[0, 0] = new_state


def _ret_sample_kernel(q_ref, k_ref, v_ref, s0_ref, d_ref, qd_ref, kd_ref, sd_ref, gw_ref, gb_ref,
                       o_ref, sn_ref, st_ref):
    st_ref[...] = s0_ref[0, 0]
    sn_ref[0, 0] = _ret_chunk(q_ref, k_ref, v_ref, d_ref, qd_ref, kd_ref, sd_ref, gw_ref, gb_ref, o_ref, st_ref)


def ret_prompt(proj, batch, gn_w, gn_b):
    t = proj.shape[0]
    n = t // batch
    nc = n // RET_CHUNK
    c = RET_CHUNK
    dmat, qdec, kdec, sdec = _ret_tables(c, c)
    hh = RET_HEADS
    tab = lambda r: pl.BlockSpec((1, r, 128), lambda b, h, j: (h, 0, 0))
    return pl.pallas_call(
        functools.partial(_ret_prompt_kernel, n_chunks=nc),
        grid=(batch, hh, nc),
        in_specs=[pl.BlockSpec((c, 256), lambda b, h, j: (b * nc + j, h)),
                  pl.BlockSpec((c, 256), lambda b, h, j: (b * nc + j, hh + h)),
                  pl.BlockSpec((c, 256), lambda b, h, j: (b * nc + j, 2 * hh + h)),
                  pl.BlockSpec((1, c, c), lambda b, h, j: (h, 0, 0)),
                  tab(c), tab(c), tab(8),
                  pl.BlockSpec((1, 256), lambda b, h, j: (0, h)),
                  pl.BlockSpec((1, 256), lambda b, h, j: (0, h))],
        out_specs=[pl.BlockSpec((c, 256), lambda b, h, j: (b * nc + j, h)),
                   pl.BlockSpec((1, 1, RET_DK, 256), lambda b, h, j: (b, h, 0, 0))],
        out_shape=[jax.ShapeDtypeStruct((t, 1024), F32),
                   jax.ShapeDtypeStruct((batch, hh, RET_DK, 256), F32)],
        scratch_shapes=[pltpu.VMEM((RET_DK, 256), F32)],
        compiler_params=_cparams("parallel", "parallel", "arbitrary"),
        name="ret_prompt",
    )(proj, proj, proj, dmat, qdec, kdec, sdec, gn_w.reshape(1, -1), gn_b.reshape(1, -1))


def ret_sample(proj3, state, gn_w, gn_b, n_tok):
    b, cp = proj3.shape[0], proj3.shape[1]
    dmat, qdec, kdec, sdec = _ret_tables(n_tok, cp)
    hh = RET_HEADS
    tab = lambda r: pl.BlockSpec((1, r, 128), lambda s, h: (h, 0, 0))
    return pl.pallas_call(
        _ret_sample_kernel,
        grid=(b, hh),
        in_specs=[pl.BlockSpec((1, cp, 256), lambda s, h: (s, 0, h)),
                  pl.BlockSpec((1, cp, 256), lambda s, h: (s, 0, hh + h)),
                  pl.BlockSpec((1, cp, 256), lambda s, h: (s, 0, 2 * hh + h)),
                  pl.BlockSpec((1, 1, RET_DK, 256), lambda s, h: (s, h, 0, 0)),
                  pl.BlockSpec((1, cp, cp), lambda s, h: (h, 0, 0)),
                  tab(cp), tab(cp), tab(8),
                  pl.BlockSpec((1, 256), lambda s, h: (0, h)),
                  pl.BlockSpec((1, 256), lambda s, h: (0, h))],
        out_specs=[pl.BlockSpec((1, cp, 256), lambda s, h: (s, 0, h)),
                   pl.BlockSpec((1, 1, RET_DK, 256), lambda s, h: (s, h, 0, 0))],
        out_shape=[jax.ShapeDtypeStruct((b, cp, 1024), F32),
                   jax.ShapeDtypeStruct((b, hh, RET_DK, 256), F32)],
        scratch_shapes=[pltpu.VMEM((RET_DK, 256), F32)],
        compiler_params=_cparams("parallel", "parallel"),
        name="ret_sample",
    )(proj3, proj3, proj3, state, dmat, qdec, kdec, sdec, gn_w.reshape(1, -1), gn_b.reshape(1, -1))


HB = 4


LANES = 128


LOG2E = 1.4426950408889634
ALIBI_COLS = 6


def _low_half(ref, h):
    col = ref[:, (h // 2) * LANES:(h // 2 + 1) * LANES]
    return pltpu.roll(col, HEAD_DIM, 1) if h % 2 else col


def _flash_kernel(q_ref, tab_ref, k_ref, v_ref, *rest, tq, tk, n_stack, window, masked):
    if masked:
        mask_ref, o_ref, m_ref, acc_ref = rest
    else:
        o_ref, m_ref, acc_ref = rest
    g = pl.program_id(1)
    q0 = pl.program_id(2) * tq
    n_loop = HB // n_stack
    lane = lax.broadcasted_iota(jnp.int32, (1, LANES), 1)
    qpos = q0 + lax.broadcasted_iota(jnp.int32, (1, tq, 1), 1)
    lo = jnp.maximum(q0 - window + 1, 0) // tk if window else 0
    hi = (q0 + tq + tk - 1) // tk

    for i in range(n_loop):
        heads = [i * n_stack + r for r in range(n_stack)]
        q_st = jnp.concatenate(
            [jnp.where(lane < HEAD_DIM, _low_half(q_ref, h) * (ATTN_SCALE * LOG2E), tab_ref[pl.ds(g * HB + h, 1), :])
             for h in heads], axis=0).astype(BF16)
        if masked:
            q_st = jnp.concatenate([q_st, jnp.concatenate([mask_ref[0, i]] * n_stack, axis=0)], axis=1)
        m_ref[...] = jnp.full(m_ref.shape, NEG_INF, F32)
        acc_ref[...] = jnp.zeros(acc_ref.shape, F32)

        def tile(j, edge):
            k0 = pl.multiple_of(j * tk, tk)
            s = _nt(q_st, k_ref[0, i, pl.ds(k0, tk), :])
            if edge:
                kpos = k0 + lax.broadcasted_iota(jnp.int32, (1, 1, tk), 2)
                ok = kpos <= qpos
                if window:
                    ok = ok & ((qpos - kpos) < window)
                s = jnp.where(ok, s.reshape(n_stack, tq, tk), NEG_INF).reshape(n_stack * tq, tk)
            m_prev = m_ref[...]
            m_new = jnp.maximum(m_prev, jnp.max(s, axis=-1, keepdims=True))
            p = jnp.exp2(s - jnp.concatenate([m_new] * (tk // LANES), axis=1))
            acc_ref[...] = (jnp.exp2(m_prev - m_new) * acc_ref[...]
                            + _nn(p.astype(BF16), v_ref[0, i, pl.ds(k0, tk), :]))
            m_ref[...] = m_new

        def edge_body(j, carry):
            tile(j, True)
            return carry

        def inner_body(j, carry):
            tile(j, False)
            return carry

        if window:
            lax.fori_loop(lo, hi, edge_body, 0)
        else:
            lax.fori_loop(lo, hi - 1, inner_body, 0)
            tile(hi - 1, True)
        acc = acc_ref[...]
        o = acc[:, :HEAD_DIM] / acc[:, HEAD_DIM:HEAD_DIM + 1]
        for r, h in enumerate(heads):
            o_ref[:, h * HEAD_DIM:(h + 1) * HEAD_DIM] = o[r * tq:(r + 1) * tq]


def flash_prompt(proj, slope_tab, k_aug, v_aug, batch, mask=None, window=0, tq=128, tk=512):
    seq = k_aug.shape[2]
    assert tk % tq == 0 or window, "the causal edge is taken to be the last key tile only"
    n_groups = N_HEADS // HB
    nq = seq // tq
    n_stack = HB if k_aug.shape[1] == n_groups else 1
    kvb = HB // n_stack
    rows = n_stack * tq
    masked = mask is not None
    in_specs = [pl.BlockSpec((tq, HB * HEAD_DIM), lambda b, g, i: (b * nq + i, g)),
                pl.BlockSpec(slope_tab.shape, lambda b, g, i: (0, 0)),
                pl.BlockSpec((1, kvb) + k_aug.shape[2:], lambda b, g, i: (b, g, 0, 0)),
                pl.BlockSpec((1, kvb, seq, LANES), lambda b, g, i: (b, g, 0, 0))]
    args = [proj, slope_tab, k_aug, v_aug]
    if masked:
        in_specs.append(pl.BlockSpec((1, kvb, tq, LANES), lambda b, g, i: (b, g, i, 0)))
        args.append(mask)
    kern = functools.partial(_flash_kernel, tq=tq, tk=tk, n_stack=n_stack, window=window, masked=masked)
    return pl.pallas_call(
        kern,
        grid=(batch, n_groups, nq),
        in_specs=in_specs,
        out_specs=pl.BlockSpec((tq, HB * HEAD_DIM), lambda b, g, i: (b * nq + i, g)),
        out_shape=jax.ShapeDtypeStruct((batch * seq, N_HEADS * HEAD_DIM), F32),
        scratch_shapes=[pltpu.VMEM((rows, LANES), F32), pltpu.VMEM((rows, LANES), F32)],
        compiler_params=_cparams("parallel", "parallel", "arbitrary"),
        name="flash_prompt",
    )(*args)


def _pack_kernel(x_ref, tab_ref, *rest, with_onehot):
    if with_onehot:
        oh_ref, o_ref = rest
    else:
        o_ref, = rest
    lane = lax.broadcasted_iota(jnp.int32, (1, LANES), 1)
    for h in range(HB):
        row = jnp.where(lane < HEAD_DIM, _low_half(x_ref, h), tab_ref[...]).astype(BF16)
        o_ref[0, h] = jnp.concatenate([row, oh_ref[...]], axis=1) if with_onehot else row


def pack_heads(proj, col_blk0, n_col_blks, batch, tab, onehot=None, ts=512):
    t = proj.shape[0]
    seq = t // batch
    ns = seq // ts
    with_onehot = onehot is not None
    width = 2 * LANES if with_onehot else LANES
    in_specs = [pl.BlockSpec((ts, HB * HEAD_DIM), lambda b, c, i: (b * ns + i, col_blk0 + c)),
                pl.BlockSpec((ts, LANES), lambda b, c, i: (i, 0))]
    args = [proj, tab]
    if with_onehot:
        in_specs.append(pl.BlockSpec((ts, LANES), lambda b, c, i: (i, 0)))
        args.append(onehot)
    return pl.pallas_call(
        functools.partial(_pack_kernel, with_onehot=with_onehot),
        grid=(batch, n_col_blks, ns),
        in_specs=in_specs,
        out_specs=pl.BlockSpec((1, HB, ts, width), lambda b, c, i: (b, c, i, 0)),
        out_shape=jax.ShapeDtypeStruct((batch, HB * n_col_blks, seq, width), BF16),
        compiler_params=_cparams("parallel", "parallel", "parallel"),
        name="pack_heads",
    )(*args)


def _rank_select(v, cand, blk, n_blocks, n_top):
    v = jnp.where(cand, v, -jnp.inf)
    rank = jnp.zeros(v.shape, jnp.int32)
    for i in range(n_blocks):
        vi = v[i:i + 1, :]
        ahead = (vi > v) | ((vi == v) & (blk > i))
        rank = rank + ahead.astype(jnp.int32)
    return cand & (rank < n_top)


def _moba_select_kernel(q_ref, km_ref, o_ref, *, tq, n_blocks):
    q0 = pl.program_id(2) * tq
    own = (q0 + lax.broadcasted_iota(jnp.int32, (1, tq), 1)) // MOBA_BLOCK
    blk = lax.broadcasted_iota(jnp.int32, (n_blocks, 1), 0)
    for r in range(HB):
        q = q_ref[:, r * HEAD_DIM:(r + 1) * HEAD_DIM].astype(BF16)
        score = _nt(km_ref[0, r], q)
        sel = _rank_select(score, blk < own, blk, n_blocks, MOBA_TOP) | (blk == own)
        o_ref[0, r] = _mask_rows(sel, n_blocks)


def _mask_rows(sel, n_blocks):
    eye = (lax.broadcasted_iota(jnp.int32, (n_blocks, 1), 0)
           == lax.broadcasted_iota(jnp.int32, (1, LANES), 1)).astype(BF16)
    kept = _tn(jnp.where(sel, 1.0, 0.0).astype(BF16), eye)
    lane = lax.broadcasted_iota(jnp.int32, (1, LANES), 1)
    return jnp.where((lane < n_blocks) & (kept < 0.5), NEG_INF, 0.0).astype(BF16)


def moba_select(proj, kmean, batch, tq=256):
    nb = kmean.shape[2]
    seq = proj.shape[0] // batch
    nq = seq // tq
    return pl.pallas_call(
        functools.partial(_moba_select_kernel, tq=tq, n_blocks=nb),
        grid=(batch, N_HEADS // HB, nq),
        in_specs=[pl.BlockSpec((tq, HB * HEAD_DIM), lambda b, g, i: (b * nq + i, g)),
                  pl.BlockSpec((1, HB, nb, HEAD_DIM), lambda b, g, i: (b, g, 0, 0))],
        out_specs=pl.BlockSpec((1, HB, tq, LANES), lambda b, g, i: (b, g, i, 0)),
        out_shape=jax.ShapeDtypeStruct((batch, N_HEADS, seq, LANES), BF16),
        compiler_params=_cparams("parallel", "parallel", "parallel"),
        name="moba_select",
    )(proj, kmean)


def _block_mean_kernel(k_ref, o_ref):
    o_ref[0] = jnp.mean(k_ref[...], axis=0, keepdims=True)


def block_mean(proj, col_blk, rows):
    t = proj.shape[0]
    return pl.pallas_call(
        _block_mean_kernel,
        grid=(t // rows,),
        in_specs=[pl.BlockSpec((rows, 1024), lambda i: (i, col_blk))],
        out_specs=pl.BlockSpec((1, 1, 1024), lambda i: (i, 0, 0)),
        out_shape=jax.ShapeDtypeStruct((t // rows, 1, 1024), F32),
        compiler_params=_cparams("parallel"),
        name="block_mean",
    )(proj)


def _nsa_cmp_kernel(slopes_ref, q_ref, kc_ref, vc_ref, o_ref, sel_ref, *, tq, n_blocks):
    g = pl.program_id(1)
    q0 = pl.program_id(2) * tq
    kc = kc_ref[0, 0]
    vc = vc_ref[0, 0]
    qpos_l = q0 + lax.broadcasted_iota(jnp.int32, (1, tq), 1)
    blk_s = lax.broadcasted_iota(jnp.int32, (n_blocks, 1), 0)
    own_l = qpos_l // CMP_BLOCK
    valid_t = blk_s < own_l
    dist_t = (qpos_l - (blk_s * CMP_BLOCK + (CMP_BLOCK - 1))).astype(F32)
    imp_t = jnp.zeros((n_blocks, tq), F32)
    for r in range(HB):
        slope = slopes_ref[g * HB + r]
        q = (q_ref[:, r * HEAD_DIM:(r + 1) * HEAD_DIM] * ATTN_SCALE).astype(BF16)
        s_t = jnp.where(valid_t, _nt(kc, q) - slope * dist_t, NEG_INF)
        e_t = jnp.where(valid_t, jnp.exp(s_t - jnp.max(s_t, axis=0, keepdims=True)), 0.0)
        p_t = e_t / jnp.maximum(jnp.sum(e_t, axis=0, keepdims=True), 1e-30)
        imp_t = imp_t + p_t
        o_ref[:, r * HEAD_DIM:(r + 1) * HEAD_DIM] = _tn(p_t.astype(BF16), vc)
    sel = _rank_select(imp_t, valid_t, blk_s, n_blocks, SEL_TOP) | (blk_s == own_l)
    sel_ref[0, 0] = _mask_rows(sel, n_blocks)


def nsa_cmp_prompt(proj, kc, vc, slopes, batch, tq=256):
    nb = kc.shape[2]
    seq = proj.shape[0] // batch
    nq = seq // tq
    return pl.pallas_call(
        functools.partial(_nsa_cmp_kernel, tq=tq, n_blocks=nb),
        grid_spec=pltpu.PrefetchScalarGridSpec(
            num_scalar_prefetch=1,
            grid=(batch, NSA_KV_HEADS, nq),
            in_specs=[pl.BlockSpec((tq, HB * HEAD_DIM), lambda b, g, i, s: (b * nq + i, g)),
                      pl.BlockSpec((1, 1, nb, HEAD_DIM), lambda b, g, i, s: (b, g, 0, 0)),
                      pl.BlockSpec((1, 1, nb, HEAD_DIM), lambda b, g, i, s: (b, g, 0, 0))],
            out_specs=[pl.BlockSpec((tq, HB * HEAD_DIM), lambda b, g, i, s: (b * nq + i, g)),
                       pl.BlockSpec((1, 1, tq, LANES), lambda b, g, i, s: (b, g, i, 0))]),
        out_shape=[jax.ShapeDtypeStruct((batch * seq, N_HEADS * HEAD_DIM), F32),
                   jax.ShapeDtypeStruct((batch, NSA_KV_HEADS, seq, LANES), BF16)],
        compiler_params=_cparams("parallel", "parallel", "parallel"),
        name="nsa_cmp_prompt",
    )(slopes, proj, kc, vc)


SEQ_PER_BATCH = 8
SEQ_PER_STEP = 4


def _compress_kernel(pt_ref, *refs, n_pages, kchunk):
    page_refs = refs[:n_pages]
    pe_ref, w1_ref, w2_ref, o_ref, stage_ref = refs[n_pages:]
    s = pl.program_id(0)
    slot = s % SEQ_PER_BATCH
    bpp = page_refs[0].shape[1]
    for p in range(n_pages):
        stage_ref[slot, p * bpp:(p + 1) * bpp, :] = page_refs[p][0]

    @pl.when(slot == SEQ_PER_BATCH - 1)
    def _():
        rows = SEQ_PER_BATCH * n_pages * bpp
        width = stage_ref.shape[-1]
        hid = jnp.zeros((rows, w1_ref.shape[1]), F32)
        for c in range(width // kchunk):
            x = stage_ref[:, :, c * kchunk:(c + 1) * kchunk].reshape(rows, kchunk)
            x = (x + pe_ref[:, c * kchunk:(c + 1) * kchunk]).astype(BF16)
            hid = hid + _nn(x, w1_ref[c * kchunk:(c + 1) * kchunk, :])
        act = (hid * _sigmoid(hid)).astype(BF16)
        o_ref[...] = _nn(act, w2_ref[...])


def compress_paged(pool2, page_table, pe_big, w1_big, w2_big):
    n_seq, n_pages = page_table.shape
    bpp, width = pool2.shape[1], pool2.shape[2]
    rows = SEQ_PER_BATCH * n_pages * bpp
    gw = w1_big.shape[1]
    in_specs = [pl.BlockSpec((1, bpp, width), functools.partial(lambda s, pt, p: (pt[s, p], 0, 0), p=p))
                for p in range(n_pages)]
    in_specs += [pl.BlockSpec((1, width), lambda s, pt: (0, 0)),
                 pl.BlockSpec((width, gw), lambda s, pt: (0, 0)),
                 pl.BlockSpec((gw, gw), lambda s, pt: (0, 0))]
    return pl.pallas_call(
        functools.partial(_compress_kernel, n_pages=n_pages, kchunk=2048),
        grid_spec=pltpu.PrefetchScalarGridSpec(
            num_scalar_prefetch=1,
            grid=(n_seq,),
            in_specs=in_specs,
            out_specs=pl.BlockSpec((rows, gw), lambda s, pt: (s // SEQ_PER_BATCH, 0)),
            scratch_shapes=[pltpu.VMEM((SEQ_PER_BATCH, n_pages * bpp, width), F32)]),
        out_shape=jax.ShapeDtypeStruct((n_seq * n_pages * bpp, gw), F32),
        compiler_params=_cparams("arbitrary"),
        name="compress_paged",
    )(page_table, *([pool2] * n_pages), pe_big, w1_big, w2_big)


COLS = 128


def _nsa_sample_cmp_kernel(qbd_ref, kc_ref, vc_ref, cs_ref, cp_ref, o_ref, mask_ref, *, n_blocks):
    qbd = qbd_ref[0]
    kc = kc_ref[0].astype(BF16)
    vc = vc_ref[0].astype(BF16)
    slope = cs_ref[0:1, :]
    qpos = cp_ref[0:1, :]
    blk = lax.broadcasted_iota(jnp.int32, (n_blocks, 1), 0)
    valid = blk < qpos // CMP_BLOCK
    dist = (qpos - (blk * CMP_BLOCK + (CMP_BLOCK - 1))).astype(F32)
    s = jnp.where(valid, _nn(kc, qbd) - slope * dist, NEG_INF)
    e = jnp.where(valid, jnp.exp(s - jnp.max(s, axis=0, keepdims=True)), 0.0)
    p = e / jnp.maximum(jnp.sum(e, axis=0, keepdims=True), 1e-30)
    o_ref[0] = _tn(p.astype(BF16), vc)
    tot = p + pltpu.roll(p, 4, 1) + pltpu.roll(p, 8, 1) + pltpu.roll(p, 12, 1)
    col = lax.broadcasted_iota(jnp.int32, (1, COLS), 1)
    tot = jnp.where(col % 16 >= 12, tot, 0.0)
    imp = tot + pltpu.roll(tot, COLS - 4, 1) + pltpu.roll(tot, COLS - 8, 1) + pltpu.roll(tot, COLS - 12, 1)
    sel = _rank_select(imp, valid, blk, n_blocks, SEL_TOP)
    mask_ref[0] = jnp.where(sel, 0.0, NEG_INF).astype(BF16)


def nsa_sample_cmp(qbd, kc, vc, col_slope, col_pos):
    n_seq, nb, c = kc.shape
    return pl.pallas_call(
        functools.partial(_nsa_sample_cmp_kernel, n_blocks=nb),
        grid=(n_seq,),
        in_specs=[pl.BlockSpec((1, c, COLS), lambda s: (s, 0, 0)),
                  pl.BlockSpec((1, nb, c), lambda s: (s, 0, 0)),
                  pl.BlockSpec((1, nb, c), lambda s: (s, 0, 0)),
                  pl.BlockSpec((8, COLS), lambda s: (0, 0)),
                  pl.BlockSpec((8, COLS), lambda s: (0, 0))],
        out_specs=[pl.BlockSpec((1, COLS, c), lambda s: (s, 0, 0)),
                   pl.BlockSpec((1, nb, COLS), lambda s: (s, 0, 0))],
        out_shape=[jax.ShapeDtypeStruct((n_seq, COLS, c), F32),
                   jax.ShapeDtypeStruct((n_seq, nb, COLS), BF16)],
        compiler_params=_cparams("parallel"),
        name="nsa_sample_cmp",
    )(qbd, kc, vc, col_slope, col_pos)


def _rank_select_lanes(v, cand, n_blocks, n_top):
    lane = lax.broadcasted_iota(jnp.int32, (1, v.shape[1]), 1)
    v = jnp.where(cand, v, -jnp.inf)
    rank = jnp.zeros(v.shape, jnp.int32)
    for i in range(n_blocks):
        vi = v[:, i:i + 1]
        ahead = (vi > v) | ((vi == v) & (lane > i))
        rank = rank + ahead.astype(jnp.int32)
    return cand & (rank < n_top)


def _sample_attn_kernel(pt_ref, q_ref, *refs, mode, n_chunks, kpos0, n_new):
    kt_refs = refs[:n_chunks]
    vt_refs = refs[n_chunks:2 * n_chunks]
    knew_ref, vnew_ref, rs_ref, rp_ref = refs[2 * n_chunks:2 * n_chunks + 4]
    pos = 2 * n_chunks + 4
    if mode == "sel":
        mask_ref, e_ref = refs[pos], refs[pos + 1]
        pos += 2
    o_ref, st_ref = refs[pos], refs[pos + 1]
    del pt_ref
    q = q_ref[0]
    slope = rs_ref[...]
    qpos = rp_ref[...]
    lane = lax.broadcasted_iota(jnp.int32, (1, PAGE_SIZE), 1)
    n_past = n_chunks * PAGE_SIZE

    if mode == "sel":
        sel_bias = _tn(mask_ref[0], e_ref[...])
    if mode == "moba":
        per_blk = MOBA_BLOCK // PAGE_SIZE
        n_blk = n_chunks // per_blk
        kmean = jnp.zeros((q.shape[1], PAGE_SIZE), F32)
        for j in range(n_blk):
            tot = kt_refs[j * per_blk][0]
            for i in range(1, per_blk):
                tot = tot + kt_refs[j * per_blk + i][0]
            mean_j = jnp.sum(tot, axis=1, keepdims=True) * (1.0 / MOBA_BLOCK)
            kmean = jnp.where(lane == j, mean_j, kmean)
        keep = _rank_select_lanes(_nn(q, kmean.astype(BF16)), lane < qpos // MOBA_BLOCK, n_blk, MOBA_TOP)
        moba_bias = jnp.where(keep, 0.0, NEG_INF)

    for p in range(n_chunks):
        kpos = kpos0 + p * PAGE_SIZE + lane
        s = _nn(q, kt_refs[p][0].astype(BF16)) - slope * (qpos - kpos).astype(F32)
        if mode == "sel":
            s = s + sel_bias[:, p * PAGE_SIZE:(p + 1) * PAGE_SIZE]
        elif mode == "moba":
            s = s + moba_bias[:, p // per_blk:p // per_blk + 1]
        else:
            diff = qpos - kpos
            s = jnp.where((kpos >= 0) & (diff >= 0) & (diff < WINDOW), s, NEG_INF)
        st_ref[:, p * PAGE_SIZE:(p + 1) * PAGE_SIZE] = s
    kpos_new = kpos0 + n_past + lane
    s_new = _nn(q, knew_ref[0].astype(BF16)) - slope * (qpos - kpos_new).astype(F32)
    st_ref[:, n_past:] = jnp.where((lane < n_new) & (kpos_new <= qpos), s_new, NEG_INF)

    s_all = st_ref[...]
    e = jnp.exp(s_all - jnp.max(s_all, axis=1, keepdims=True))
    p_all = (e / jnp.sum(e, axis=1, keepdims=True)).astype(BF16)
    o = _nt(p_all[:, n_past:], vnew_ref[0].astype(BF16))
    for p in range(n_chunks):
        o = o + _nt(p_all[:, p * PAGE_SIZE:(p + 1) * PAGE_SIZE], vt_refs[p][0].astype(BF16))
    o_ref[0] = o


def sample_attn(mode, q_rows, kt_src, vt_src, page_table, kt_new, vt_new, row_slope, row_pos, kpos0, n_new,
                page0=0, win_chunks=0, mask=None, block_onehot=None):
    n_seq, rows, c = q_rows.shape
    if mode == "win":
        n_chunks = win_chunks
        src_spec = [pl.BlockSpec((1, c, PAGE_SIZE), functools.partial(lambda s, pt, p: (page0 + s, 0, p), p=p))
                    for p in range(n_chunks)]
    else:
        n_chunks = page_table.shape[1]
        src_spec = [pl.BlockSpec((1, c, PAGE_SIZE), functools.partial(
            lambda s, pt, p: (page0 + pt[s, p], 0, 0), p=p)) for p in range(n_chunks)]
    in_specs = [pl.BlockSpec((1, rows, c), lambda s, pt: (s, 0, 0))] + src_spec + src_spec
    in_specs += [pl.BlockSpec((1, c, PAGE_SIZE), lambda s, pt: (s, 0, 0)),
                 pl.BlockSpec((1, c, PAGE_SIZE), lambda s, pt: (s, 0, 0)),
                 pl.BlockSpec((rows, PAGE_SIZE), lambda s, pt: (0, 0)),
                 pl.BlockSpec((rows, PAGE_SIZE), lambda s, pt: (0, 0))]
    args = [q_rows] + [kt_src] * n_chunks + [vt_src] * n_chunks + [kt_new, vt_new, row_slope, row_pos]
    if mode == "sel":
        nb = mask.shape[1]
        in_specs += [pl.BlockSpec((1, nb, rows), lambda s, pt: (s, 0, 0)),
                     pl.BlockSpec(block_onehot.shape, lambda s, pt: (0, 0))]
        args += [mask, block_onehot]
    kern = functools.partial(_sample_attn_kernel, mode=mode, n_chunks=n_chunks, kpos0=kpos0, n_new=n_new)
    return pl.pallas_call(
        kern,
        grid_spec=pltpu.PrefetchScalarGridSpec(
            num_scalar_prefetch=1,
            grid=(n_seq,),
            in_specs=in_specs,
            out_specs=pl.BlockSpec((1, rows, c), lambda s, pt: (s, 0, 0)),
            scratch_shapes=[pltpu.VMEM((rows, (n_chunks + 1) * PAGE_SIZE), F32)]),
        out_shape=jax.ShapeDtypeStruct((n_seq, rows, c), F32),
        compiler_params=_cparams("parallel"),
        name="sample_attn_" + mode,
    )(page_table, *args)


def _compress_native_kernel(pt_ref, *refs, n_pages, n_groups):
    n_in = SEQ_PER_STEP * n_pages
    page_refs = refs[:n_in]
    pe_ref, w1_ref, w2_ref, o_ref, stage_ref = refs[n_in:]
    del pt_ref
    steps_per_batch = SEQ_PER_BATCH // SEQ_PER_STEP
    slot = pl.program_id(0) % steps_per_batch
    page_rows = n_groups * HEAD_DIM
    for ip in range(n_in):
        start = pl.multiple_of((slot * n_in + ip) * page_rows, page_rows)
        stage_ref[pl.ds(start, page_rows), :] = page_refs[ip][0]

    @pl.when(slot == steps_per_batch - 1)
    def _():
        n_rows = SEQ_PER_BATCH * n_pages * n_groups
        hid = jnp.zeros((n_rows, PAGE_SIZE), F32)
        for dd in range(HEAD_DIM // 2):
            x0 = stage_ref[pl.ds(2 * dd, n_rows, stride=HEAD_DIM), :] + pe_ref[2 * dd:2 * dd + 1, :]
            x1 = stage_ref[pl.ds(2 * dd + 1, n_rows, stride=HEAD_DIM), :] + pe_ref[2 * dd + 1:2 * dd + 2, :]
            hid = hid + _nn(jnp.concatenate([x0, x1], axis=1).astype(BF16), w1_ref[dd])
        act = (hid * _sigmoid(hid)).astype(BF16)
        o_ref[...] = _nn(act, w2_ref[...])


def compress_native(pool_t, page_table, pe_t, w1_pairs, w2_blocks):
    n_seq, n_pages = page_table.shape
    page_rows = pool_t.shape[1]
    n_groups = page_rows // HEAD_DIM
    out_rows = SEQ_PER_BATCH * n_pages * n_groups
    in_specs = [pl.BlockSpec((1, page_rows, PAGE_SIZE), functools.partial(
        lambda s, pt, i, p: (pt[s * SEQ_PER_STEP + i, p], 0, 0), i=i, p=p))
        for i in range(SEQ_PER_STEP) for p in range(n_pages)]
    in_specs += [pl.BlockSpec(pe_t.shape, lambda s, pt: (0, 0)),
                 pl.BlockSpec(w1_pairs.shape, lambda s, pt: (0, 0, 0)),
                 pl.BlockSpec(w2_blocks.shape, lambda s, pt: (0, 0))]
    steps_per_batch = SEQ_PER_BATCH // SEQ_PER_STEP
    return pl.pallas_call(
        functools.partial(_compress_native_kernel, n_pages=n_pages, n_groups=n_groups),
        grid_spec=pltpu.PrefetchScalarGridSpec(
            num_scalar_prefetch=1,
            grid=(n_seq // SEQ_PER_STEP,),
            in_specs=in_specs,
            out_specs=pl.BlockSpec((out_rows, PAGE_SIZE), lambda s, pt: (s // steps_per_batch, 0)),
            scratch_shapes=[pltpu.VMEM((SEQ_PER_BATCH * n_pages * page_rows, PAGE_SIZE), F32)]),
        out_shape=jax.ShapeDtypeStruct((n_seq * n_pages * n_groups, PAGE_SIZE), F32),
        compiler_params=_cparams("arbitrary"),
        name="compress_native",
    )(page_table, *([pool_t] * (SEQ_PER_STEP * n_pages)), pe_t, w1_pairs, w2_blocks)


NSA_KV_W = NSA_KV_HEADS * HEAD_DIM
NSA_PROJ_W = 3840
NSA_Z_BLK, NSA_KV_BLK0, NSA_GL_BLK = 1, 8, 14


def _split3(x):
    hi = x.astype(BF16).astype(F32)
    mid = (x - hi).astype(BF16).astype(F32)
    return hi, mid, (x - hi - mid).astype(BF16).astype(F32)


def _lane_table(rows, cols):
    body = jnp.stack(cols, axis=-1)
    return jnp.pad(body, ((0, 0), (HEAD_DIM, LANES - HEAD_DIM - len(cols))))


def _flash_tables(slopes, seq):
    assert seq <= 64 * 256, "key positions are split into two bf16-exact pieces"
    hi, mid, lo = _split3(slopes * LOG2E)
    pos = jnp.arange(seq)
    pos_hi, pos_lo = (pos // 64 * 64).astype(F32), (pos % 64).astype(F32)
    return (_lane_table(N_HEADS, [hi, hi, mid, mid, lo, lo]),
            _lane_table(seq, [pos_hi, pos_lo] * (ALIBI_COLS // 2)),
            _lane_table(seq, [jnp.ones((seq,), F32)]))


def _block_onehot(seq, block):
    return (jnp.arange(seq)[:, None] // block == jnp.arange(LANES)[None, :]).astype(BF16)


def _column_tables(slopes, past, n_tok):
    col = jnp.arange(COLS)
    live = col < N_HEADS * n_tok
    head = jnp.minimum(col // n_tok, N_HEADS - 1)
    slope = jnp.where(live, slopes[head], 0.0)
    pos = jnp.where(live, past + col % n_tok, past).astype(jnp.int32)
    grp = jnp.where(live, head // NSA_GROUP, -1).astype(jnp.int32)
    ints = jnp.concatenate([pos[None, :], grp[None, :], jnp.zeros((6, COLS), jnp.int32)], axis=0)
    return jnp.broadcast_to(slope[None, :], (8, COLS)), ints


def _row_tables(col_slope, col_ints):
    return (jnp.broadcast_to(col_slope[0][:, None], (COLS, PAGE_SIZE)),
            jnp.broadcast_to(col_ints[0][:, None], (COLS, PAGE_SIZE)))


def _new_rows_t(t2d, n_seq, n_tok):
    rows_t = t2d.reshape(n_seq, n_tok, -1).transpose(0, 2, 1)
    return jnp.pad(rows_t, ((0, 0), (0, 0), (0, PAGE_SIZE - n_tok)))


def _stored_tiles(cache):
    nd = cache.ndim
    t = jnp.transpose(cache, tuple(range(nd - 3)) + (nd - 2, nd - 1, nd - 3))
    return t.reshape((-1, cache.shape[-2] * cache.shape[-1], cache.shape[-3]))


def _expand_query(q2d, n_seq, n_tok, heads_per_key):
    q4 = (q2d * ATTN_SCALE).reshape(n_seq, n_tok, N_HEADS, HEAD_DIM)
    n_keys = N_HEADS // heads_per_key
    owner = (jnp.arange(N_HEADS)[:, None] // heads_per_key == jnp.arange(n_keys)[None, :])
    qbd = jnp.where(owner[None, None, :, None, :], q4[..., None], 0.0)
    qbd = qbd.transpose(0, 4, 3, 2, 1).reshape(n_seq, n_keys * HEAD_DIM, N_HEADS * n_tok)
    return jnp.pad(qbd, ((0, 0), (0, 0), (0, COLS - N_HEADS * n_tok))).astype(BF16)


def _take_own(o_t, n_seq, n_tok, heads_per_key):
    n_keys = N_HEADS // heads_per_key
    o5 = o_t[:, :N_HEADS * n_tok].reshape(n_seq, N_HEADS, n_tok, n_keys, HEAD_DIM)
    hh = jnp.arange(N_HEADS)
    own = o5[:, hh, :, hh // heads_per_key, :]
    return own.transpose(1, 2, 0, 3).reshape(n_seq * n_tok, N_HEADS * HEAD_DIM)


def _pad_rows(t2d, n_seq, n_tok):
    return jnp.pad(t2d.reshape(n_seq, n_tok, -1), ((0, 0), (0, 8 - n_tok), (0, 0)))


def _nsa_weights(w_in, pe_k, pe_v, w1_k, w2_k, w1_v, w2_v):
    d = w_in.shape[0]
    q, kv, gl, z = (w_in[:, :1024], w_in[:, 1024:1024 + 6 * NSA_KV_W],
                    w_in[:, 1024 + 6 * NSA_KV_W:1024 + 6 * NSA_KV_W + 3 * N_HEADS], w_in[:, -1024:])
    pad = jnp.zeros((d, NSA_PROJ_W - (2048 + 6 * NSA_KV_W + 3 * N_HEADS)), w_in.dtype)
    w_re = jnp.concatenate([q, z, kv, gl, pad], axis=1).astype(BF16)
    eye = jnp.eye(NSA_KV_HEADS, dtype=w1_k.dtype)

    def big(pe, w1, w2):
        hid = w1.shape[1]
        w1r = w1.reshape(CMP_BLOCK, HEAD_DIM, hid)
        w1b = (w1r[:, None, :, None, :] * eye[None, :, None, :, None]).reshape(
            CMP_BLOCK * NSA_KV_W, NSA_KV_HEADS * hid).astype(BF16)
        w2b = (w2[None, :, None, :] * eye[:, None, :, None]).reshape(
            NSA_KV_HEADS * hid, NSA_KV_W).astype(BF16)
        peb = jnp.broadcast_to(pe[:, None, :], (CMP_BLOCK, NSA_KV_HEADS, HEAD_DIM)).reshape(1, -1)
        return peb, w1b, w2b

    bpp = PAGE_SIZE // CMP_BLOCK
    eye_b = jnp.eye(bpp, dtype=w1_k.dtype)

    def stored(pe, w1, w2):
        hid = w1.shape[1]
        w1r = w1.reshape(CMP_BLOCK, HEAD_DIM, hid).transpose(1, 0, 2)
        w1d = (w1r[:, None, :, None, :] * eye_b[None, :, None, :, None]).reshape(
            HEAD_DIM, bpp * CMP_BLOCK, bpp * hid)
        w1p = w1d.reshape(HEAD_DIM // 2, 2 * bpp * CMP_BLOCK, bpp * hid).astype(BF16)
        w2b = (w2[None, :, None, :] * eye_b[:, None, :, None]).reshape(bpp * hid, bpp * HEAD_DIM).astype(BF16)
        return jnp.tile(pe.T, (1, bpp)), w1p, w2b

    return (w_re, big(pe_k, w1_k, w2_k), big(pe_v, w1_v, w2_v), stored(pe_k, w1_k, w2_k),
            stored(pe_v, w1_v, w2_v))


def _nsa_prompt(x, nw, w_re, cmp_k, cmp_v, w_out, slopes, batch, final_w):
    t = x.shape[0]
    n = t // batch
    proj = norm_matmul(x, nw, w_re)
    kv = [proj[:, 2048 + NSA_KV_W * i:2048 + NSA_KV_W * (i + 1)] for i in range(6)]
    ck, cv, sk, sv, wk, wv = kv
    row_w = CMP_BLOCK * NSA_KV_W
    bpp = PAGE_SIZE // CMP_BLOCK
    n_pages = t // PAGE_SIZE
    pt = jnp.arange(n_pages, dtype=jnp.int32).reshape(SEQ_PER_BATCH, n_pages // SEQ_PER_BATCH)
    nb = n // CMP_BLOCK
    comp = lambda rows, wts: compress_paged(rows.reshape(n_pages, bpp, row_w), pt, *wts).reshape(
        batch, nb, NSA_KV_HEADS, HEAD_DIM).transpose(0, 2, 1, 3).astype(BF16)
    o_cmp, mask = nsa_cmp_prompt(proj, comp(ck, cmp_k), comp(cv, cmp_v), slopes, batch)
    slope_tab, pos_tab, ones_tab = _flash_tables(slopes, n)
    blk = NSA_KV_BLK0
    o_sel = flash_prompt(proj, slope_tab, pack_heads(proj, blk + 2, 1, batch, pos_tab, _block_onehot(n, CMP_BLOCK)),
                         pack_heads(proj, blk + 3, 1, batch, ones_tab), batch, mask=mask, tq=128, tk=1024)
    o_win = flash_prompt(proj, slope_tab, pack_heads(proj, blk + 4, 1, batch, pos_tab),
                         pack_heads(proj, blk + 5, 1, batch, ones_tab), batch, window=WINDOW, tq=256, tk=256)
    y = out_proj(x, proj, NSA_Z_BLK, [o_cmp, o_sel, o_win], w_out, gl_blk=NSA_GL_BLK, final_w=final_w)
    st = lambda a: a.reshape(batch, n, NSA_KV_HEADS, HEAD_DIM)
    keep = min(WINDOW, n)
    return y, (st(ck), st(cv), st(sk), st(sv), st(wk)[:, n - keep:], st(wv)[:, n - keep:])


def _nsa_sample(x, nw, w_re, cmp_k, cmp_v, w_out, slopes, n_seq, final_w, page_table, layer_j, caches):
    ck_t, cv_t, sk_t, sv_t, wk_t, wv_t, win_k, win_v, n_pool = caches
    n_tok = x.shape[0] // n_seq
    n_pages = page_table.shape[1]
    past = n_pages * PAGE_SIZE
    win_len = win_k.shape[2]
    bpp = PAGE_SIZE // CMP_BLOCK
    proj = norm_matmul(x, nw, w_re)
    kv = [proj[:, 2048 + NSA_KV_W * i:2048 + NSA_KV_W * (i + 1)] for i in range(6)]
    ck, cv, sk, sv, wk, wv = kv
    pt_layer = page_table + layer_j * n_pool
    comp = lambda pool_t, wts: compress_native(pool_t, pt_layer, *wts).reshape(
        n_seq, n_pages, NSA_KV_HEADS, bpp, HEAD_DIM).transpose(0, 1, 3, 2, 4).reshape(
        n_seq, n_pages * bpp, NSA_KV_W)
    col_slope, col_ints = _column_tables(slopes, past, n_tok)
    qbd = _expand_query(proj[:, :1024], n_seq, n_tok, NSA_GROUP)
    o_cmp, mask = nsa_sample_cmp(qbd, comp(ck_t, cmp_k), comp(cv_t, cmp_v), col_slope, col_ints)
    q_rows = qbd.transpose(0, 2, 1)
    row_slope, row_pos = _row_tables(col_slope, col_ints)
    new_t = lambda a: _new_rows_t(a, n_seq, n_tok)
    block_onehot = (jnp.arange(past // CMP_BLOCK)[:, None] == jnp.arange(past)[None, :] // CMP_BLOCK).astype(BF16)
    o_sel = sample_attn("sel", q_rows, sk_t, sv_t, page_table, new_t(sk), new_t(sv), row_slope, row_pos, 0, n_tok,
                        page0=layer_j * n_pool, mask=mask, block_onehot=block_onehot)
    o_win = sample_attn("win", q_rows, wk_t, wv_t, page_table, new_t(wk), new_t(wv), row_slope, row_pos,
                        past - win_len, n_tok, page0=layer_j * n_seq, win_chunks=win_len // PAGE_SIZE)
    own = lambda o: _take_own(o, n_seq, n_tok, NSA_GROUP)
    y = out_proj(x, proj, NSA_Z_BLK, [own(o_cmp), own(o_sel), own(o_win)], w_out, gl_blk=NSA_GL_BLK,
                 final_w=final_w)
    st = lambda a: a.reshape(n_seq, n_tok, NSA_KV_HEADS, HEAD_DIM)
    kw = jnp.concatenate([win_k[layer_j], st(wk)], axis=1)
    vw = jnp.concatenate([win_v[layer_j], st(wv)], axis=1)
    keep = min(WINDOW, kw.shape[1])
    return y, (st(ck), st(cv), st(sk), st(sv), kw[:, kw.shape[1] - keep:], vw[:, vw.shape[1] - keep:])


def _nsa_cache_views(cmp_k, cmp_v, sel_k, sel_v, win_k, win_v):
    n_pool = cmp_k.shape[1]
    return (_stored_tiles(cmp_k), _stored_tiles(cmp_v), _stored_tiles(sel_k), _stored_tiles(sel_v),
            _stored_tiles(win_k), _stored_tiles(win_v), win_k, win_v, n_pool)


def _moba_prompt(x, nw, w_in, w_out, slopes, batch, final_w):
    t = x.shape[0]
    n = t // batch
    proj = norm_matmul(x, nw, w_in)
    k, v = proj[:, 1024:2048], proj[:, 2048:3072]
    nf = n // MOBA_BLOCK
    kmean = block_mean(proj, 1, MOBA_BLOCK).reshape(batch, nf, N_HEADS, HEAD_DIM).transpose(0, 2, 1, 3).astype(BF16)
    mask = moba_select(proj, kmean, batch)
    slope_tab, pos_tab, ones_tab = _flash_tables(slopes, n)
    o = flash_prompt(proj, slope_tab, pack_heads(proj, 4, 4, batch, pos_tab, _block_onehot(n, MOBA_BLOCK)),
                     pack_heads(proj, 8, 4, batch, ones_tab), batch, mask=mask, tq=512, tk=1024)
    y = out_proj(x, proj, 3, [o], w_out, final_w=final_w)
    st = lambda a: a.reshape(batch, n, N_HEADS, HEAD_DIM)
    return y, (st(k), st(v))


def _moba_sample(x, nw, w_in, w_out, slopes, n_seq, final_w, page_table, k_pool, v_pool):
    n_tok = x.shape[0] // n_seq
    past = page_table.shape[1] * PAGE_SIZE
    assert past % MOBA_BLOCK == 0 and n_tok <= MOBA_BLOCK
    proj = norm_matmul(x, nw, w_in)
    k, v = proj[:, 1024:2048], proj[:, 2048:3072]
    q_rows = _expand_query(proj[:, :1024], n_seq, n_tok, 1).transpose(0, 2, 1)
    row_slope, row_pos = _row_tables(*_column_tables(slopes, past, n_tok))
    o = sample_attn("moba", q_rows, _stored_tiles(k_pool), _stored_tiles(v_pool), page_table,
                    _new_rows_t(k, n_seq, n_tok), _new_rows_t(v, n_seq, n_tok), row_slope, row_pos, 0, n_tok)
    y = out_proj(x, proj, 3, [_take_own(o, n_seq, n_tok, 1)], w_out, final_w=final_w)
    st = lambda a: a.reshape(n_seq, n_tok, N_HEADS, HEAD_DIM)
    return y, (st(k), st(v))


def _ret_prompt_layer(x, nw, w_in, gn_w, gn_b, w_out, batch, final_w):
    proj = norm_matmul(x, nw, w_in)
    on, state = ret_prompt(proj, batch, gn_w, gn_b)
    return out_proj(x, proj, 3, [on], w_out, final_w=final_w), state


def _ret_sample_layer(x, nw, w_in, gn_w, gn_b, w_out, n_seq, final_w, state):
    n_tok = x.shape[0] // n_seq
    proj = norm_matmul(x, nw, w_in)
    on, new_state = ret_sample(_pad_rows(proj, n_seq, n_tok), state, gn_w, gn_b, n_tok)
    on = on[:, :n_tok].reshape(n_seq * n_tok, -1)
    return out_proj(x, proj, 3, [on], w_out, final_w=final_w), new_state


def kernel(x_prompt, x_sample, cache_nsa_cmp_k, cache_nsa_cmp_v, cache_nsa_sel_k, cache_nsa_sel_v,
           cache_nsa_win_k, cache_nsa_win_v, cache_moba_k, cache_moba_v, state_ret, page_table,
           norm_w, final_norm_w, nsa_w_in, nsa_pe_k, nsa_pe_v, nsa_w1_k, nsa_w2_k, nsa_w1_v, nsa_w2_v,
           nsa_w_out, moba_w_in, moba_w_out, ret_w_in, ret_gn_w, ret_gn_b, ret_w_out):
    batch, seq, d = x_prompt.shape
    n_seq, n_tok, _ = x_sample.shape
    depth = norm_w.shape[0]
    slopes = jnp.exp2(-8.0 * (jnp.arange(N_HEADS, dtype=F32) + 1.0) / N_HEADS)
    xp = x_prompt.reshape(batch * seq, d)
    xs = x_sample.reshape(n_seq * n_tok, d)
    nsa_p, nsa_s, moba_p, moba_s, ret_p, ret_s = [], [], [], [], [], []
    nsa_caches = _nsa_cache_views(cache_nsa_cmp_k, cache_nsa_cmp_v, cache_nsa_sel_k, cache_nsa_sel_v,
                                  cache_nsa_win_k, cache_nsa_win_v)
    for layer in range(depth):
        j = layer // N_MIXERS
        fw = final_norm_w if layer == depth - 1 else None
        nw = norm_w[layer]
        if layer % N_MIXERS == 0:
            w_re, cmp_k, cmp_v, cmp_k_st, cmp_v_st = _nsa_weights(
                nsa_w_in[j], nsa_pe_k[j], nsa_pe_v[j], nsa_w1_k[j], nsa_w2_k[j], nsa_w1_v[j], nsa_w2_v[j])
            w_out = nsa_w_out[j].astype(BF16)
            xp, stp = _nsa_prompt(xp, nw, w_re, cmp_k, cmp_v, w_out, slopes, batch, fw)
            xs, sts = _nsa_sample(xs, nw, w_re, cmp_k_st, cmp_v_st, w_out, slopes, n_seq, fw, page_table, j,
                                  nsa_caches)
            nsa_p.append(stp)
            nsa_s.append(sts)
        elif layer % N_MIXERS == 1:
            w_in, w_out = moba_w_in[j].astype(BF16), moba_w_out[j].astype(BF16)
            xp, stp = _moba_prompt(xp, nw, w_in, w_out, slopes, batch, fw)
            xs, sts = _moba_sample(xs, nw, w_in, w_out, slopes, n_seq, fw, page_table, cache_moba_k[j],
                                   cache_moba_v[j])
            moba_p.append(stp)
            moba_s.append(sts)
        else:
            w_in, w_out = ret_w_in[j].astype(BF16), ret_w_out[j].astype(BF16)
            xp, stp = _ret_prompt_layer(xp, nw, w_in, ret_gn_w[j], ret_gn_b[j], w_out, batch, fw)
            xs, sts = _ret_sample_layer(xs, nw, w_in, ret_gn_w[j], ret_gn_b[j], w_out, n_seq, fw, state_ret[j])
            ret_p.append(stp)
            ret_s.append(sts)
    st = lambda items, i: jnp.stack([s[i] for s in items])
    return (xp.reshape(batch, seq, d), xs.reshape(n_seq, n_tok, d),
            st(nsa_p, 0), st(nsa_p, 1), st(nsa_p, 2), st(nsa_p, 3), st(nsa_p, 4), st(nsa_p, 5),
            st(moba_p, 0), st(moba_p, 1), jnp.stack(ret_p),
            st(nsa_s, 0), st(nsa_s, 1), st(nsa_s, 2), st(nsa_s, 3), st(nsa_s, 4), st(nsa_s, 5),
            st(moba_s, 0), st(moba_s, 1), jnp.stack(ret_s))
```

```python
import functools

import jax
import jax.numpy as jnp
import numpy as np
from jax import lax
from jax.experimental import pallas as pl
from jax.experimental.pallas import tpu as pltpu

F32 = jnp.float32
BF16 = jnp.bfloat16

HEAD_DIM = 64
N_HEADS = 16
NSA_KV_HEADS = 4
NSA_GROUP = 4
CMP_BLOCK = 64
SEL_TOP = 15
WINDOW = 512
MOBA_BLOCK = 256
MOBA_TOP = 3
RET_HEADS = 4
RET_DK = 256
RET_CHUNK = 128
PAGE_SIZE = 128
RMS_EPS = 1e-6
GN_EPS = 1e-5
NEG_INF = -1e30
ATTN_SCALE = HEAD_DIM ** -0.5
N_MIXERS = 3

VMEM_LIMIT = 56 * 1024 * 1024


def _cparams(*sem):
    return pltpu.CompilerParams(dimension_semantics=sem, vmem_limit_bytes=VMEM_LIMIT)


def _nt(a, b):
    return lax.dot_general(a, b, (((1,), (1,)), ((), ())), preferred_element_type=F32)


def _nn(a, b):
    return lax.dot_general(a, b, (((1,), (0,)), ((), ())), preferred_element_type=F32)


def _tn(a, b):
    return lax.dot_general(a, b, (((0,), (0,)), ((), ())), preferred_element_type=F32)


def _sigmoid(x):
    return 1.0 / (1.0 + jnp.exp(-x))


def _norm_matmul_kernel(x_ref, nw_ref, w_ref, o_ref, xn_ref):
    @pl.when(pl.program_id(1) == 0)
    def _():
        x = x_ref[...]
        ms = jnp.mean(x * x, axis=-1, keepdims=True)
        xn_ref[...] = ((x * lax.rsqrt(ms + RMS_EPS)) * nw_ref[...]).astype(BF16)

    o_ref[...] = _nn(xn_ref[...], w_ref[...])


def norm_matmul(x, nw, w_bf16):
    t, d = x.shape
    n = w_bf16.shape[1]
    tm = min(t, 1024)
    tn = 1024 if n % 1024 == 0 else 768
    assert n % tn == 0
    return pl.pallas_call(
        _norm_matmul_kernel,
        grid=(t // tm, n // tn),
        in_specs=[pl.BlockSpec((tm, d), lambda i, j: (i, 0)),
                  pl.BlockSpec((1, d), lambda i, j: (0, 0)),
                  pl.BlockSpec((d, tn), lambda i, j: (0, j))],
        out_specs=pl.BlockSpec((tm, tn), lambda i, j: (i, j)),
        out_shape=jax.ShapeDtypeStruct((t, n), F32),
        scratch_shapes=[pltpu.VMEM((tm, d), BF16)],
        compiler_params=_cparams("parallel", "arbitrary"),
        name="norm_matmul",
    )(x, nw.reshape(1, d), w_bf16)


def _out_proj_kernel(*refs, n_o, nsa_gates, final_norm):
    x_ref, z_ref = refs[0], refs[1]
    o_refs = refs[2:2 + n_o]
    pos = 2 + n_o
    if nsa_gates:
        gl_ref = refs[pos]
        pos += 1
    w_ref = refs[pos]
    pos += 1
    if final_norm:
        fw_ref = refs[pos]
        pos += 1
    y_ref = refs[pos]

    if nsa_gates:
        gates = _sigmoid(gl_ref[...])
        tm = gates.shape[0]
        lane = lax.broadcasted_iota(jnp.int32, (tm, 128), 1)
        cols = []
        for vb in range(N_HEADS // 2):
            acc = None
            for br in range(n_o):
                c0 = (2 * vb) * 3 + br
                c1 = (2 * vb + 1) * 3 + br
                g = jnp.where(lane < HEAD_DIM,
                              jnp.broadcast_to(gates[:, c0:c0 + 1], (tm, 128)),
                              jnp.broadcast_to(gates[:, c1:c1 + 1], (tm, 128)))
                term = g * o_refs[br][:, vb * 128:(vb + 1) * 128]
                acc = term if acc is None else acc + term
            cols.append(acc)
        o = jnp.concatenate(cols, axis=1)
    else:
        o = o_refs[0][...]
    z = z_ref[...]
    gated = (o * (z * _sigmoid(z))).astype(BF16)
    y = x_ref[...] + _nn(gated, w_ref[...])
    if final_norm:
        ms = jnp.mean(y * y, axis=-1, keepdims=True)
        y = (y * lax.rsqrt(ms + RMS_EPS)) * fw_ref[...]
    y_ref[...] = y


def out_proj(x, proj, z_blk, o_list, w_bf16, gl_blk=None, final_w=None, tm=256):
    t, d = x.shape
    tm = min(tm, t)
    n_o = len(o_list)
    in_specs = [pl.BlockSpec((tm, d), lambda i: (i, 0)),
                pl.BlockSpec((tm, d), lambda i: (i, z_blk))]
    args = [x, proj]
    for o in o_list:
        in_specs.append(pl.BlockSpec((tm, d), lambda i: (i, 0)))
        args.append(o)
    if gl_blk is not None:
        in_specs.append(pl.BlockSpec((tm, 256), lambda i: (i, gl_blk)))
        args.append(proj)
    in_specs.append(pl.BlockSpec((d, d), lambda i: (0, 0)))
    args.append(w_bf16)
    if final_w is not None:
        in_specs.append(pl.BlockSpec((1, d), lambda i: (0, 0)))
        args.append(final_w.reshape(1, d))
    kern = functools.partial(_out_proj_kernel, n_o=n_o, nsa_gates=gl_blk is not None,
                             final_norm=final_w is not None)
    return pl.pallas_call(
        kern,
        grid=(t // tm,),
        in_specs=in_specs,
        out_specs=pl.BlockSpec((tm, d), lambda i: (i, 0)),
        out_shape=jax.ShapeDtypeStruct((t, d), F32),
        compiler_params=_cparams("parallel"),
        name="out_proj",
    )(*args)


def _ret_tables(c_true, c_pad):
    lg = jnp.log1p(-jnp.exp2(-5.0 - jnp.arange(RET_HEADS, dtype=F32)))[:, None, None]
    i = jnp.arange(c_pad, dtype=F32)
    live = (i < c_true)
    diff = i[:, None] - i[None, :]
    dmat = jnp.where((diff >= 0) & live[:, None] & live[None, :], jnp.exp(lg * jnp.maximum(diff, 0.0)), 0.0)
    qdec = jnp.exp(lg * (i[:, None] + 1.0)) * jnp.ones((1, 1, 128), F32)
    kdec = jnp.where(live[:, None], jnp.exp(lg * (c_true - 1.0 - i[:, None])), 0.0) * jnp.ones((1, 1, 128), F32)
    sdec = jnp.exp(lg * c_true) * jnp.ones((1, 8, 128), F32)
    return dmat, qdec, kdec, sdec


def _ret_chunk(q, k, v, state, dmat, qdec, kdec, sdec, gw, gb):
    k = k * (RET_DK ** -0.5)
    qb, kb, vb = q.astype(BF16), k.astype(BF16), v.astype(BF16)
    inner = _nt(qb, kb) * dmat
    o = _nn(inner.astype(BF16), vb) + _nn(qb, state.astype(BF16)) * qdec
    kw = (k * kdec).astype(BF16)
    new_state = sdec * state + _nn(kw.T, vb)

    mu = jnp.mean(o, axis=-1, keepdims=True)
    var = jnp.mean(jnp.square(o - mu), axis=-1, keepdims=True)
    on = (o - mu) * lax.rsqrt(var + GN_EPS)
    return on * gw + gb, new_state


def _ret_prompt_kernel(q_ref, k_ref, v_ref, d_ref, qd_ref, kd_ref, sd_ref, gw_ref, gb_ref,
                       o_ref, sn_ref, st_ref, *, n_chunks):
    c = pl.program_id(2)

    @pl.when(c == 0)
    def _():
        st_ref[...] = jnp.zeros_like(st_ref)

    on, new_state = _ret_chunk(q_ref[...], k_ref[...], v_ref[...], st_ref[...], d_ref[0], qd_ref[0][:, 0:1],
                               kd_ref[0][:, 0:1], sd_ref[0][0:1, 0:1], gw_ref[...], gb_ref[...])
    o_ref[...] = on
    st_ref[...] = new_state

    @pl.when(c == n_chunks - 1)
    def _():
        sn_ref[0, 0] = new_state


def _ret_sample_kernel(x_ref, s0_ref, d_ref, qd_ref, kd_ref, sd_ref, gw_ref, gb_ref, o_ref, sn_ref):
    w = RET_DK
    for h in range(RET_HEADS):
        cols = lambda blk: x_ref[0, :, (blk * RET_HEADS + h) * w:(blk * RET_HEADS + h + 1) * w]
        on, new_state = _ret_chunk(cols(0), cols(1), cols(2), s0_ref[0, h], d_ref[h], qd_ref[h][:, 0:1],
                                   kd_ref[h][:, 0:1], sd_ref[h][0:1, 0:1], gw_ref[:, h * w:(h + 1) * w],
                                   gb_ref[:, h * w:(h + 1) * w])
        o_ref[0, :, h * w:(h + 1) * w] = on
        sn_ref[0, h] = new_state


def ret_prompt(proj, batch, gn_w, gn_b):
    t = proj.shape[0]
    n = t // batch
    nc = n // RET_CHUNK
    c = RET_CHUNK
    dmat, qdec, kdec, sdec = _ret_tables(c, c)
    hh = RET_HEADS
    tab = lambda r: pl.BlockSpec((1, r, 128), lambda b, h, j: (h, 0, 0))
    return pl.pallas_call(
        functools.partial(_ret_prompt_kernel, n_chunks=nc),
        grid=(batch, hh, nc),
        in_specs=[pl.BlockSpec((c, 256), lambda b, h, j: (b * nc + j, h)),
                  pl.BlockSpec((c, 256), lambda b, h, j: (b * nc + j, hh + h)),
                  pl.BlockSpec((c, 256), lambda b, h, j: (b * nc + j, 2 * hh + h)),
                  pl.BlockSpec((1, c, c), lambda b, h, j: (h, 0, 0)),
                  tab(c), tab(c), tab(8),
                  pl.BlockSpec((1, 256), lambda b, h, j: (0, h)),
                  pl.BlockSpec((1, 256), lambda b, h, j: (0, h))],
        out_specs=[pl.BlockSpec((c, 256), lambda b, h, j: (b * nc + j, h)),
                   pl.BlockSpec((1, 1, RET_DK, 256), lambda b, h, j: (b, h, 0, 0))],
        out_shape=[jax.ShapeDtypeStruct((t, 1024), F32),
                   jax.ShapeDtypeStruct((batch, hh, RET_DK, 256), F32)],
        scratch_shapes=[pltpu.VMEM((RET_DK, 256), F32)],
        compiler_params=_cparams("parallel", "parallel", "arbitrary"),
        name="ret_prompt",
    )(proj, proj, proj, dmat, qdec, kdec, sdec, gn_w.reshape(1, -1), gn_b.reshape(1, -1))


def ret_sample(proj3, state, gn_w, gn_b, n_tok):
    b, cp = proj3.shape[0], proj3.shape[1]
    dmat, qdec, kdec, sdec = _ret_tables(n_tok, cp)
    hh = RET_HEADS
    whole = lambda a: pl.BlockSpec(a.shape, lambda s: (0,) * a.ndim)
    gw, gb = gn_w.reshape(1, -1), gn_b.reshape(1, -1)
    return pl.pallas_call(
        _ret_sample_kernel,
        grid=(b,),
        in_specs=[pl.BlockSpec((1, cp, proj3.shape[2]), lambda s: (s, 0, 0)),
                  pl.BlockSpec((1, hh, RET_DK, 256), lambda s: (s, 0, 0, 0)),
                  whole(dmat), whole(qdec), whole(kdec), whole(sdec), whole(gw), whole(gb)],
        out_specs=[pl.BlockSpec((1, cp, hh * 256), lambda s: (s, 0, 0)),
                   pl.BlockSpec((1, hh, RET_DK, 256), lambda s: (s, 0, 0, 0))],
        out_shape=[jax.ShapeDtypeStruct((b, cp, hh * 256), F32),
                   jax.ShapeDtypeStruct((b, hh, RET_DK, 256), F32)],
        compiler_params=_cparams("parallel"),
        name="ret_sample",
    )(proj3, state, dmat, qdec, kdec, sdec, gw, gb)


HB = 4


LANES = 128


LOG2E = 1.4426950408889634
ALIBI_COLS = 6


def _low_half(ref, h):
    col = ref[:, (h // 2) * LANES:(h // 2 + 1) * LANES]
    return pltpu.roll(col, HEAD_DIM, 1) if h % 2 else col


def _flash_kernel(q_ref, tab_ref, k_ref, v_ref, *rest, tq, tk, n_stack, window, masked):
    if masked:
        mask_ref, o_ref, m_ref, acc_ref = rest
    else:
        o_ref, m_ref, acc_ref = rest
    g = pl.program_id(1)
    q0 = pl.program_id(2) * tq
    n_loop = HB // n_stack
    lane = lax.broadcasted_iota(jnp.int32, (1, LANES), 1)
    qpos = q0 + lax.broadcasted_iota(jnp.int32, (1, tq, 1), 1)
    lo = jnp.maximum(q0 - window + 1, 0) // tk if window else 0
    hi = (q0 + tq + tk - 1) // tk

    for i in range(n_loop):
        heads = [i * n_stack + r for r in range(n_stack)]
        q_st = jnp.concatenate(
            [jnp.where(lane < HEAD_DIM, _low_half(q_ref, h) * (ATTN_SCALE * LOG2E), tab_ref[pl.ds(g * HB + h, 1), :])
             for h in heads], axis=0).astype(BF16)
        if masked:
            q_st = jnp.concatenate([q_st, jnp.concatenate([mask_ref[0, i]] * n_stack, axis=0)], axis=1)
        m_ref[...] = jnp.full(m_ref.shape, NEG_INF, F32)
        acc_ref[...] = jnp.zeros(acc_ref.shape, F32)

        def tile(k0, width, edge):
            s = _nt(q_st, k_ref[0, i, pl.ds(k0, width), :])
            if edge:
                kpos = k0 + lax.broadcasted_iota(jnp.int32, (1, 1, width), 2)
                ok = kpos <= qpos
                if window:
                    ok = ok & ((qpos - kpos) < window)
                s = jnp.where(ok, s.reshape(n_stack, tq, width), NEG_INF).reshape(n_stack * tq, width)
            m_prev = m_ref[...]
            m_new = jnp.maximum(m_prev, jnp.max(s, axis=-1, keepdims=True))
            p = jnp.exp2(s - jnp.concatenate([m_new] * (width // LANES), axis=1))
            acc_ref[...] = (jnp.exp2(m_prev - m_new) * acc_ref[...]
                            + _nn(p.astype(BF16), v_ref[0, i, pl.ds(k0, width), :]))
            m_ref[...] = m_new

        def edge_body(j, carry):
            tile(pl.multiple_of(j * tk, tk), tk, True)
            return carry

        def inner_body(j, carry):
            tile(pl.multiple_of(j * tk, tk), tk, False)
            return carry

        if window:
            lax.fori_loop(lo, hi, edge_body, 0)
        else:
            lax.fori_loop(lo, hi - 1, inner_body, 0)
            tile(pl.multiple_of((hi - 1) * tk, tk), tk, True)
        acc = acc_ref[...]
        o = acc[:, :HEAD_DIM] / acc[:, HEAD_DIM:HEAD_DIM + 1]
        for r, h in enumerate(heads):
            o_ref[:, h * HEAD_DIM:(h + 1) * HEAD_DIM] = o[r * tq:(r + 1) * tq]


def flash_prompt(proj, slope_tab, k_aug, v_aug, batch, mask=None, window=0, tq=128, tk=512):
    seq = k_aug.shape[2]
    assert tk % tq == 0 or window, "the causal edge is taken to be the last key tile only"
    n_groups = N_HEADS // HB
    nq = seq // tq
    n_stack = HB if k_aug.shape[1] == n_groups else 1
    kvb = HB // n_stack
    rows = n_stack * tq
    masked = mask is not None
    in_specs = [pl.BlockSpec((tq, HB * HEAD_DIM), lambda b, g, i: (b * nq + i, g)),
                pl.BlockSpec(slope_tab.shape, lambda b, g, i: (0, 0)),
                pl.BlockSpec((1, kvb) + k_aug.shape[2:], lambda b, g, i: (b, g, 0, 0)),
                pl.BlockSpec((1, kvb, seq, LANES), lambda b, g, i: (b, g, 0, 0))]
    args = [proj, slope_tab, k_aug, v_aug]
    if masked:
        in_specs.append(pl.BlockSpec((1, kvb, tq, LANES), lambda b, g, i: (b, g, i, 0)))
        args.append(mask)
    kern = functools.partial(_flash_kernel, tq=tq, tk=tk, n_stack=n_stack, window=window, masked=masked)
    return pl.pallas_call(
        kern,
        grid=(batch, n_groups, nq),
        in_specs=in_specs,
        out_specs=pl.BlockSpec((tq, HB * HEAD_DIM), lambda b, g, i: (b * nq + i, g)),
        out_shape=jax.ShapeDtypeStruct((batch * seq, N_HEADS * HEAD_DIM), F32),
        scratch_shapes=[pltpu.VMEM((rows, LANES), F32), pltpu.VMEM((rows, LANES), F32)],
        compiler_params=_cparams("parallel", "parallel", "arbitrary"),
        name="flash_prompt",
    )(*args)


def _pack_kernel(x_ref, tab_ref, *rest, with_onehot):
    if with_onehot:
        oh_ref, o_ref = rest
    else:
        o_ref, = rest
    lane = lax.broadcasted_iota(jnp.int32, (1, LANES), 1)
    for h in range(HB):
        row = jnp.where(lane < HEAD_DIM, _low_half(x_ref, h), tab_ref[...]).astype(BF16)
        o_ref[0, h] = jnp.concatenate([row, oh_ref[...]], axis=1) if with_onehot else row


def pack_heads(proj, col_blk0, n_col_blks, batch, tab, onehot=None, ts=512):
    t = proj.shape[0]
    seq = t // batch
    ns = seq // ts
    with_onehot = onehot is not None
    width = 2 * LANES if with_onehot else LANES
    in_specs = [pl.BlockSpec((ts, HB * HEAD_DIM), lambda b, c, i: (b * ns + i, col_blk0 + c)),
                pl.BlockSpec((ts, LANES), lambda b, c, i: (i, 0))]
    args = [proj, tab]
    if with_onehot:
        in_specs.append(pl.BlockSpec((ts, LANES), lambda b, c, i: (i, 0)))
        args.append(onehot)
    return pl.pallas_call(
        functools.partial(_pack_kernel, with_onehot=with_onehot),
        grid=(batch, n_col_blks, ns),
        in_specs=in_specs,
        out_specs=pl.BlockSpec((1, HB, ts, width), lambda b, c, i: (b, c, i, 0)),
        out_shape=jax.ShapeDtypeStruct((batch, HB * n_col_blks, seq, width), BF16),
        compiler_params=_cparams("parallel", "parallel", "parallel"),
        name="pack_heads",
    )(*args)


def _rank_select(v, cand, blk, n_blocks, n_top):
    v = jnp.where(cand, v, -jnp.inf)
    rank = jnp.zeros(v.shape, jnp.int32)
    for i in range(n_blocks):
        vi = v[i:i + 1, :]
        ahead = (vi > v) | ((vi == v) & (blk > i))
        rank = rank + ahead.astype(jnp.int32)
    return cand & (rank < n_top)


def _moba_select_kernel(q_ref, km_ref, o_ref, *, tq, n_blocks):
    q0 = pl.program_id(2) * tq
    own = (q0 + lax.broadcasted_iota(jnp.int32, (1, tq), 1)) // MOBA_BLOCK
    blk = lax.broadcasted_iota(jnp.int32, (n_blocks, 1), 0)
    for r in range(HB):
        q = q_ref[:, r * HEAD_DIM:(r + 1) * HEAD_DIM].astype(BF16)
        score = _nt(km_ref[0, r], q)
        sel = _rank_select(score, blk < own, blk, n_blocks, MOBA_TOP) | (blk == own)
        o_ref[0, r] = _mask_rows(sel, n_blocks)


def _mask_rows(sel, n_blocks):
    eye = (lax.broadcasted_iota(jnp.int32, (n_blocks, 1), 0)
           == lax.broadcasted_iota(jnp.int32, (1, LANES), 1)).astype(BF16)
    kept = _tn(jnp.where(sel, 1.0, 0.0).astype(BF16), eye)
    lane = lax.broadcasted_iota(jnp.int32, (1, LANES), 1)
    return jnp.where((lane < n_blocks) & (kept < 0.5), NEG_INF, 0.0).astype(BF16)


def moba_select(proj, kmean, batch, tq=256):
    nb = kmean.shape[2]
    seq = proj.shape[0] // batch
    nq = seq // tq
    return pl.pallas_call(
        functools.partial(_moba_select_kernel, tq=tq, n_blocks=nb),
        grid=(batch, N_HEADS // HB, nq),
        in_specs=[pl.BlockSpec((tq, HB * HEAD_DIM), lambda b, g, i: (b * nq + i, g)),
                  pl.BlockSpec((1, HB, nb, HEAD_DIM), lambda b, g, i: (b, g, 0, 0))],
        out_specs=pl.BlockSpec((1, HB, tq, LANES), lambda b, g, i: (b, g, i, 0)),
        out_shape=jax.ShapeDtypeStruct((batch, N_HEADS, seq, LANES), BF16),
        compiler_params=_cparams("parallel", "parallel", "parallel"),
        name="moba_select",
    )(proj, kmean)


def _block_mean_kernel(k_ref, o_ref):
    o_ref[0] = jnp.mean(k_ref[...], axis=0, keepdims=True)


def block_mean(proj, col_blk, rows):
    t = proj.shape[0]
    return pl.pallas_call(
        _block_mean_kernel,
        grid=(t // rows,),
        in_specs=[pl.BlockSpec((rows, 1024), lambda i: (i, col_blk))],
        out_specs=pl.BlockSpec((1, 1, 1024), lambda i: (i, 0, 0)),
        out_shape=jax.ShapeDtypeStruct((t // rows, 1, 1024), F32),
        compiler_params=_cparams("parallel"),
        name="block_mean",
    )(proj)


def _nsa_cmp_kernel(slopes_ref, q_ref, kc_ref, vc_ref, o_ref, sel_ref, *, tq, n_blocks):
    g = pl.program_id(1)
    q0 = pl.program_id(2) * tq
    kc = kc_ref[0, 0]
    vc = vc_ref[0, 0]
    qpos_l = q0 + lax.broadcasted_iota(jnp.int32, (1, tq), 1)
    blk_s = lax.broadcasted_iota(jnp.int32, (n_blocks, 1), 0)
    own_l = qpos_l // CMP_BLOCK
    valid_t = blk_s < own_l
    dist_t = (qpos_l - (blk_s * CMP_BLOCK + (CMP_BLOCK - 1))).astype(F32)
    imp_t = jnp.zeros((n_blocks, tq), F32)
    for r in range(HB):
        slope = slopes_ref[g * HB + r]
        q = (q_ref[:, r * HEAD_DIM:(r + 1) * HEAD_DIM] * ATTN_SCALE).astype(BF16)
        s_t = jnp.where(valid_t, _nt(kc, q) - slope * dist_t, NEG_INF)
        e_t = jnp.where(valid_t, jnp.exp(s_t - jnp.max(s_t, axis=0, keepdims=True)), 0.0)
        p_t = e_t / jnp.maximum(jnp.sum(e_t, axis=0, keepdims=True), 1e-30)
        imp_t = imp_t + p_t
        o_ref[:, r * HEAD_DIM:(r + 1) * HEAD_DIM] = _tn(p_t.astype(BF16), vc)
    sel = _rank_select(imp_t, valid_t, blk_s, n_blocks, SEL_TOP) | (blk_s == own_l)
    sel_ref[0, 0] = _mask_rows(sel, n_blocks)


def nsa_cmp_prompt(proj, kc, vc, slopes, batch, tq=256):
    nb = kc.shape[2]
    seq = proj.shape[0] // batch
    nq = seq // tq
    return pl.pallas_call(
        functools.partial(_nsa_cmp_kernel, tq=tq, n_blocks=nb),
        grid_spec=pltpu.PrefetchScalarGridSpec(
            num_scalar_prefetch=1,
            grid=(batch, NSA_KV_HEADS, nq),
            in_specs=[pl.BlockSpec((tq, HB * HEAD_DIM), lambda b, g, i, s: (b * nq + i, g)),
                      pl.BlockSpec((1, 1, nb, HEAD_DIM), lambda b, g, i, s: (b, g, 0, 0)),
                      pl.BlockSpec((1, 1, nb, HEAD_DIM), lambda b, g, i, s: (b, g, 0, 0))],
            out_specs=[pl.BlockSpec((tq, HB * HEAD_DIM), lambda b, g, i, s: (b * nq + i, g)),
                       pl.BlockSpec((1, 1, tq, LANES), lambda b, g, i, s: (b, g, i, 0))]),
        out_shape=[jax.ShapeDtypeStruct((batch * seq, N_HEADS * HEAD_DIM), F32),
                   jax.ShapeDtypeStruct((batch, NSA_KV_HEADS, seq, LANES), BF16)],
        compiler_params=_cparams("parallel", "parallel", "parallel"),
        name="nsa_cmp_prompt",
    )(slopes, proj, kc, vc)


SEQ_PER_BATCH = 8
SEQ_PER_STEP = 4


def _compress_kernel(pt_ref, *refs, n_pages, kchunk):
    page_refs = refs[:n_pages]
    pe_ref, w1_ref, w2_ref, o_ref, stage_ref = refs[n_pages:]
    s = pl.program_id(0)
    slot = s % SEQ_PER_BATCH
    bpp = page_refs[0].shape[1]
    for p in range(n_pages):
        stage_ref[slot, p * bpp:(p + 1) * bpp, :] = page_refs[p][0]

    @pl.when(slot == SEQ_PER_BATCH - 1)
    def _():
        rows = SEQ_PER_BATCH * n_pages * bpp
        width = stage_ref.shape[-1]
        hid = jnp.zeros((rows, w1_ref.shape[1]), F32)
        for c in range(width // kchunk):
            x = stage_ref[:, :, c * kchunk:(c + 1) * kchunk].reshape(rows, kchunk)
            x = (x + pe_ref[:, c * kchunk:(c + 1) * kchunk]).astype(BF16)
            hid = hid + _nn(x, w1_ref[c * kchunk:(c + 1) * kchunk, :])
        act = (hid * _sigmoid(hid)).astype(BF16)
        o_ref[...] = _nn(act, w2_ref[...])


def compress_paged(pool2, page_table, pe_big, w1_big, w2_big):
    n_seq, n_pages = page_table.shape
    bpp, width = pool2.shape[1], pool2.shape[2]
    rows = SEQ_PER_BATCH * n_pages * bpp
    gw = w1_big.shape[1]
    in_specs = [pl.BlockSpec((1, bpp, width), functools.partial(lambda s, pt, p: (pt[s, p], 0, 0), p=p))
                for p in range(n_pages)]
    in_specs += [pl.BlockSpec((1, width), lambda s, pt: (0, 0)),
                 pl.BlockSpec((width, gw), lambda s, pt: (0, 0)),
                 pl.BlockSpec((gw, gw), lambda s, pt: (0, 0))]
    return pl.pallas_call(
        functools.partial(_compress_kernel, n_pages=n_pages, kchunk=2048),
        grid_spec=pltpu.PrefetchScalarGridSpec(
            num_scalar_prefetch=1,
            grid=(n_seq,),
            in_specs=in_specs,
            out_specs=pl.BlockSpec((rows, gw), lambda s, pt: (s // SEQ_PER_BATCH, 0)),
            scratch_shapes=[pltpu.VMEM((SEQ_PER_BATCH, n_pages * bpp, width), F32)]),
        out_shape=jax.ShapeDtypeStruct((n_seq * n_pages * bpp, gw), F32),
        compiler_params=_cparams("arbitrary"),
        name="compress_paged",
    )(page_table, *([pool2] * n_pages), pe_big, w1_big, w2_big)


COLS = 128


def _nsa_sample_cmp_kernel(qbd_ref, kc_ref, vc_ref, cs_ref, cp_ref, o_ref, mask_ref, *, n_blocks):
    qbd = qbd_ref[0]
    kc = kc_ref[0].astype(BF16)
    vc = vc_ref[0].astype(BF16)
    slope = cs_ref[0:1, :]
    qpos = cp_ref[0:1, :]
    blk = lax.broadcasted_iota(jnp.int32, (n_blocks, 1), 0)
    valid = blk < qpos // CMP_BLOCK
    dist = (qpos - (blk * CMP_BLOCK + (CMP_BLOCK - 1))).astype(F32)
    s = jnp.where(valid, _nn(kc, qbd) - slope * dist, NEG_INF)
    e = jnp.where(valid, jnp.exp(s - jnp.max(s, axis=0, keepdims=True)), 0.0)
    p = e / jnp.maximum(jnp.sum(e, axis=0, keepdims=True), 1e-30)
    o_ref[0] = _tn(p.astype(BF16), vc)
    tot = p + pltpu.roll(p, 4, 1) + pltpu.roll(p, 8, 1) + pltpu.roll(p, 12, 1)
    col = lax.broadcasted_iota(jnp.int32, (1, COLS), 1)
    tot = jnp.where(col % 16 >= 12, tot, 0.0)
    imp = tot + pltpu.roll(tot, COLS - 4, 1) + pltpu.roll(tot, COLS - 8, 1) + pltpu.roll(tot, COLS - 12, 1)
    sel = _rank_select(imp, valid, blk, n_blocks, SEL_TOP)
    mask_ref[0] = jnp.where(sel, 0.0, NEG_INF).astype(BF16)


def nsa_sample_cmp(qbd, kc, vc, col_slope, col_pos):
    n_seq, nb, c = kc.shape
    return pl.pallas_call(
        functools.partial(_nsa_sample_cmp_kernel, n_blocks=nb),
        grid=(n_seq,),
        in_specs=[pl.BlockSpec((1, c, COLS), lambda s: (s, 0, 0)),
                  pl.BlockSpec((1, nb, c), lambda s: (s, 0, 0)),
                  pl.BlockSpec((1, nb, c), lambda s: (s, 0, 0)),
                  pl.BlockSpec((8, COLS), lambda s: (0, 0)),
                  pl.BlockSpec((8, COLS), lambda s: (0, 0))],
        out_specs=[pl.BlockSpec((1, COLS, c), lambda s: (s, 0, 0)),
                   pl.BlockSpec((1, nb, COLS), lambda s: (s, 0, 0))],
        out_shape=[jax.ShapeDtypeStruct((n_seq, COLS, c), F32),
                   jax.ShapeDtypeStruct((n_seq, nb, COLS), BF16)],
        compiler_params=_cparams("parallel"),
        name="nsa_sample_cmp",
    )(qbd, kc, vc, col_slope, col_pos)


def _rank_select_lanes(v, cand, n_blocks, n_top):
    lane = lax.broadcasted_iota(jnp.int32, (1, v.shape[1]), 1)
    v = jnp.where(cand, v, -jnp.inf)
    rank = jnp.zeros(v.shape, jnp.int32)
    for i in range(n_blocks):
        vi = v[:, i:i + 1]
        ahead = (vi > v) | ((vi == v) & (lane > i))
        rank = rank + ahead.astype(jnp.int32)
    return cand & (rank < n_top)


def _sample_attn_kernel(pt_ref, q_ref, *refs, mode, n_chunks, kpos0, n_new):
    kt_refs = refs[:n_chunks]
    vt_refs = refs[n_chunks:2 * n_chunks]
    knew_ref, vnew_ref, rs_ref, rp_ref = refs[2 * n_chunks:2 * n_chunks + 4]
    pos = 2 * n_chunks + 4
    if mode == "sel":
        mask_ref, e_ref = refs[pos], refs[pos + 1]
        pos += 2
    o_ref = refs[pos]
    if mode == "win":
        kw_ref, vw_ref = refs[pos + 1], refs[pos + 2]
    st_ref = refs[-1]
    del pt_ref
    q = q_ref[0]
    slope = rs_ref[...]
    qpos = rp_ref[...]
    lane = lax.broadcasted_iota(jnp.int32, (1, PAGE_SIZE), 1)
    n_past = n_chunks * PAGE_SIZE

    if mode == "sel":
        sel_bias = _tn(mask_ref[0], e_ref[...])
    if mode == "moba":
        per_blk = MOBA_BLOCK // PAGE_SIZE
        n_blk = n_chunks // per_blk
        kmean = jnp.zeros((q.shape[1], PAGE_SIZE), F32)
        for j in range(n_blk):
            tot = kt_refs[j * per_blk][0]
            for i in range(1, per_blk):
                tot = tot + kt_refs[j * per_blk + i][0]
            mean_j = jnp.sum(tot, axis=1, keepdims=True) * (1.0 / MOBA_BLOCK)
            kmean = jnp.where(lane == j, mean_j, kmean)
        keep = _rank_select_lanes(_nn(q, kmean.astype(BF16)), lane < qpos // MOBA_BLOCK, n_blk, MOBA_TOP)
        moba_bias = jnp.where(keep, 0.0, NEG_INF)

    for p in range(n_chunks):
        kpos = kpos0 + p * PAGE_SIZE + lane
        s = _nn(q, kt_refs[p][0].astype(BF16)) - slope * (qpos - kpos).astype(F32)
        if mode == "sel":
            s = s + sel_bias[:, p * PAGE_SIZE:(p + 1) * PAGE_SIZE]
        elif mode == "moba":
            s = s + moba_bias[:, p // per_blk:p // per_blk + 1]
        else:
            diff = qpos - kpos
            s = jnp.where((kpos >= 0) & (diff >= 0) & (diff < WINDOW), s, NEG_INF)
        st_ref[:, p * PAGE_SIZE:(p + 1) * PAGE_SIZE] = s
    kpos_new = kpos0 + n_past + lane
    s_new = _nn(q, knew_ref[0].astype(BF16)) - slope * (qpos - kpos_new).astype(F32)
    st_ref[:, n_past:] = jnp.where((lane < n_new) & (kpos_new <= qpos), s_new, NEG_INF)

    s_all = st_ref[...]
    e = jnp.exp(s_all - jnp.max(s_all, axis=1, keepdims=True))
    p_all = (e / jnp.sum(e, axis=1, keepdims=True)).astype(BF16)
    o = _nt(p_all[:, n_past:], vnew_ref[0].astype(BF16))
    for p in range(n_chunks):
        o = o + _nt(p_all[:, p * PAGE_SIZE:(p + 1) * PAGE_SIZE], vt_refs[p][0].astype(BF16))
    o_ref[0] = o
    if mode == "win":
        for src_refs, new_ref, dst_ref in ((kt_refs, knew_ref, kw_ref), (vt_refs, vnew_ref, vw_ref)):
            both = jnp.concatenate([r[0] for r in src_refs] + [new_ref[0]], axis=1)
            dst_ref[0] = both[:, n_new:n_new + n_past]


def sample_attn(mode, q_rows, kt_src, vt_src, page_table, kt_new, vt_new, row_slope, row_pos, kpos0, n_new,
                page0=0, win_chunks=0, mask=None, block_onehot=None):
    n_seq, rows, c = q_rows.shape
    if mode == "win":
        n_chunks = win_chunks
        src_spec = [pl.BlockSpec((1, c, PAGE_SIZE), functools.partial(lambda s, pt, p: (page0 + s, 0, p), p=p))
                    for p in range(n_chunks)]
    else:
        n_chunks = page_table.shape[1]
        src_spec = [pl.BlockSpec((1, c, PAGE_SIZE), functools.partial(
            lambda s, pt, p: (page0 + pt[s, p], 0, 0), p=p)) for p in range(n_chunks)]
    in_specs = [pl.BlockSpec((1, rows, c), lambda s, pt: (s, 0, 0))] + src_spec + src_spec
    in_specs += [pl.BlockSpec((1, c, PAGE_SIZE), lambda s, pt: (s, 0, 0)),
                 pl.BlockSpec((1, c, PAGE_SIZE), lambda s, pt: (s, 0, 0)),
                 pl.BlockSpec((rows, PAGE_SIZE), lambda s, pt: (0, 0)),
                 pl.BlockSpec((rows, PAGE_SIZE), lambda s, pt: (0, 0))]
    args = [q_rows] + [kt_src] * n_chunks + [vt_src] * n_chunks + [kt_new, vt_new, row_slope, row_pos]
    if mode == "sel":
        nb = mask.shape[1]
        in_specs += [pl.BlockSpec((1, nb, rows), lambda s, pt: (s, 0, 0)),
                     pl.BlockSpec(block_onehot.shape, lambda s, pt: (0, 0))]
        args += [mask, block_onehot]
    kern = functools.partial(_sample_attn_kernel, mode=mode, n_chunks=n_chunks, kpos0=kpos0, n_new=n_new)
    out_specs = [pl.BlockSpec((1, rows, c), lambda s, pt: (s, 0, 0))]
    out_shape = [jax.ShapeDtypeStruct((n_seq, rows, c), F32)]
    if mode == "win":
        win_len = n_chunks * PAGE_SIZE
        out_specs += [pl.BlockSpec((1, c, win_len), lambda s, pt: (s, 0, 0))] * 2
        out_shape += [jax.ShapeDtypeStruct((n_seq, c, win_len), F32)] * 2
    out = pl.pallas_call(
        kern,
        grid_spec=pltpu.PrefetchScalarGridSpec(
            num_scalar_prefetch=1,
            grid=(n_seq,),
            in_specs=in_specs,
            out_specs=out_specs,
            scratch_shapes=[pltpu.VMEM((rows, (n_chunks + 1) * PAGE_SIZE), F32)]),
        out_shape=out_shape,
        compiler_params=_cparams("parallel"),
        name="sample_attn_" + mode,
    )(page_table, *args)
    return out if mode == "win" else out[0]


def _compress_native_kernel(pt_ref, *refs, n_pages, n_groups):
    n_in = SEQ_PER_STEP * n_pages
    page_refs = refs[:n_in]
    pe_ref, w1_ref, w2_ref, o_ref, stage_ref = refs[n_in:]
    del pt_ref
    steps_per_batch = SEQ_PER_BATCH // SEQ_PER_STEP
    slot = pl.program_id(0) % steps_per_batch
    page_rows = n_groups * HEAD_DIM
    for ip in range(n_in):
        start = pl.multiple_of((slot * n_in + ip) * page_rows, page_rows)
        stage_ref[pl.ds(start, page_rows), :] = page_refs[ip][0]

    @pl.when(slot == steps_per_batch - 1)
    def _():
        n_rows = SEQ_PER_BATCH * n_pages * n_groups
        hid = jnp.zeros((n_rows, PAGE_SIZE), F32)
        for dd in range(HEAD_DIM // 2):
            x0 = stage_ref[pl.ds(2 * dd, n_rows, stride=HEAD_DIM), :] + pe_ref[2 * dd:2 * dd + 1, :]
            x1 = stage_ref[pl.ds(2 * dd + 1, n_rows, stride=HEAD_DIM), :] + pe_ref[2 * dd + 1:2 * dd + 2, :]
            hid = hid + _nn(jnp.concatenate([x0, x1], axis=1).astype(BF16), w1_ref[dd])
        act = (hid * _sigmoid(hid)).astype(BF16)
        o_ref[...] = _nn(act, w2_ref[...])


def compress_native(pool_t, page_table, pe_t, w1_pairs, w2_blocks):
    n_seq, n_pages = page_table.shape
    page_rows = pool_t.shape[1]
    n_groups = page_rows // HEAD_DIM
    out_rows = SEQ_PER_BATCH * n_pages * n_groups
    in_specs = [pl.BlockSpec((1, page_rows, PAGE_SIZE), functools.partial(
        lambda s, pt, i, p: (pt[s * SEQ_PER_STEP + i, p], 0, 0), i=i, p=p))
        for i in range(SEQ_PER_STEP) for p in range(n_pages)]
    in_specs += [pl.BlockSpec(pe_t.shape, lambda s, pt: (0, 0)),
                 pl.BlockSpec(w1_pairs.shape, lambda s, pt: (0, 0, 0)),
                 pl.BlockSpec(w2_blocks.shape, lambda s, pt: (0, 0))]
    steps_per_batch = SEQ_PER_BATCH // SEQ_PER_STEP
    return pl.pallas_call(
        functools.partial(_compress_native_kernel, n_pages=n_pages, n_groups=n_groups),
        grid_spec=pltpu.PrefetchScalarGridSpec(
            num_scalar_prefetch=1,
            grid=(n_seq // SEQ_PER_STEP,),
            in_specs=in_specs,
            out_specs=pl.BlockSpec((out_rows, PAGE_SIZE), lambda s, pt: (s // steps_per_batch, 0)),
            scratch_shapes=[pltpu.VMEM((SEQ_PER_BATCH * n_pages * page_rows, PAGE_SIZE), F32)]),
        out_shape=jax.ShapeDtypeStruct((n_seq * n_pages * n_groups, PAGE_SIZE), F32),
        compiler_params=_cparams("arbitrary"),
        name="compress_native",
    )(page_table, *([pool_t] * (SEQ_PER_STEP * n_pages)), pe_t, w1_pairs, w2_blocks)


NSA_KV_W = NSA_KV_HEADS * HEAD_DIM
NSA_PROJ_W = 3840
NSA_Z_BLK, NSA_KV_BLK0, NSA_GL_BLK = 1, 8, 14


def _split3(x):
    hi = x.astype(BF16).astype(F32)
    mid = (x - hi).astype(BF16).astype(F32)
    return hi, mid, (x - hi - mid).astype(BF16).astype(F32)


def _lane_table(rows, cols):
    body = jnp.stack(cols, axis=-1)
    return jnp.pad(body, ((0, 0), (HEAD_DIM, LANES - HEAD_DIM - len(cols))))


def _flash_tables(slopes, seq):
    assert seq <= 64 * 256, "key positions are split into two bf16-exact pieces"
    hi, mid, lo = _split3(slopes * LOG2E)
    pos = jnp.arange(seq)
    pos_hi, pos_lo = (pos // 64 * 64).astype(F32), (pos % 64).astype(F32)
    return (_lane_table(N_HEADS, [hi, hi, mid, mid, lo, lo]),
            _lane_table(seq, [pos_hi, pos_lo] * (ALIBI_COLS // 2)),
            _lane_table(seq, [jnp.ones((seq,), F32)]))


def _block_onehot(seq, block):
    return (jnp.arange(seq)[:, None] // block == jnp.arange(LANES)[None, :]).astype(BF16)


def _column_tables(slopes, past, n_tok):
    col = jnp.arange(COLS)
    live = col < N_HEADS * n_tok
    head = jnp.minimum(col // n_tok, N_HEADS - 1)
    slope = jnp.where(live, slopes[head], 0.0)
    pos = jnp.where(live, past + col % n_tok, past).astype(jnp.int32)
    grp = jnp.where(live, head // NSA_GROUP, -1).astype(jnp.int32)
    ints = jnp.concatenate([pos[None, :], grp[None, :], jnp.zeros((6, COLS), jnp.int32)], axis=0)
    return jnp.broadcast_to(slope[None, :], (8, COLS)), ints


def _row_tables(col_slope, col_ints):
    return (jnp.broadcast_to(col_slope[0][:, None], (COLS, PAGE_SIZE)),
            jnp.broadcast_to(col_ints[0][:, None], (COLS, PAGE_SIZE)))


def _new_rows_t(t2d, n_seq, n_tok):
    rows_t = t2d.reshape(n_seq, n_tok, -1).transpose(0, 2, 1)
    return jnp.pad(rows_t, ((0, 0), (0, 0), (0, PAGE_SIZE - n_tok)))


def _stored_tiles(cache):
    nd = cache.ndim
    t = jnp.transpose(cache, tuple(range(nd - 3)) + (nd - 2, nd - 1, nd - 3))
    return t.reshape((-1, cache.shape[-2] * cache.shape[-1], cache.shape[-3]))


def _expand_query(q2d, n_seq, n_tok, heads_per_key):
    q4 = (q2d * ATTN_SCALE).reshape(n_seq, n_tok, N_HEADS, HEAD_DIM)
    n_keys = N_HEADS // heads_per_key
    owner = (jnp.arange(N_HEADS)[:, None] // heads_per_key == jnp.arange(n_keys)[None, :])
    qbd = jnp.where(owner[None, None, :, None, :], q4[..., None], 0.0)
    qbd = qbd.transpose(0, 4, 3, 2, 1).reshape(n_seq, n_keys * HEAD_DIM, N_HEADS * n_tok)
    return jnp.pad(qbd, ((0, 0), (0, 0), (0, COLS - N_HEADS * n_tok))).astype(BF16)


def _take_own(o_t, n_seq, n_tok, heads_per_key):
    n_keys = N_HEADS // heads_per_key
    o5 = o_t[:, :N_HEADS * n_tok].reshape(n_seq, N_HEADS, n_tok, n_keys, HEAD_DIM)
    hh = jnp.arange(N_HEADS)
    own = o5[:, hh, :, hh // heads_per_key, :]
    return own.transpose(1, 2, 0, 3).reshape(n_seq * n_tok, N_HEADS * HEAD_DIM)


def _pad_rows(t2d, n_seq, n_tok):
    return jnp.pad(t2d.reshape(n_seq, n_tok, -1), ((0, 0), (0, 8 - n_tok), (0, 0)))


def _nsa_weights(w_in, pe_k, pe_v, w1_k, w2_k, w1_v, w2_v):
    d = w_in.shape[0]
    q, kv, gl, z = (w_in[:, :1024], w_in[:, 1024:1024 + 6 * NSA_KV_W],
                    w_in[:, 1024 + 6 * NSA_KV_W:1024 + 6 * NSA_KV_W + 3 * N_HEADS], w_in[:, -1024:])
    pad = jnp.zeros((d, NSA_PROJ_W - (2048 + 6 * NSA_KV_W + 3 * N_HEADS)), w_in.dtype)
    w_re = jnp.concatenate([q, z, kv, gl, pad], axis=1).astype(BF16)
    eye = jnp.eye(NSA_KV_HEADS, dtype=w1_k.dtype)

    def big(pe, w1, w2):
        hid = w1.shape[1]
        w1r = w1.reshape(CMP_BLOCK, HEAD_DIM, hid)
        w1b = (w1r[:, None, :, None, :] * eye[None, :, None, :, None]).reshape(
            CMP_BLOCK * NSA_KV_W, NSA_KV_HEADS * hid).astype(BF16)
        w2b = (w2[None, :, None, :] * eye[:, None, :, None]).reshape(
            NSA_KV_HEADS * hid, NSA_KV_W).astype(BF16)
        peb = jnp.broadcast_to(pe[:, None, :], (CMP_BLOCK, NSA_KV_HEADS, HEAD_DIM)).reshape(1, -1)
        return peb, w1b, w2b

    bpp = PAGE_SIZE // CMP_BLOCK
    eye_b = jnp.eye(bpp, dtype=w1_k.dtype)

    def stored(pe, w1, w2):
        hid = w1.shape[1]
        w1r = w1.reshape(CMP_BLOCK, HEAD_DIM, hid).transpose(1, 0, 2)
        w1d = (w1r[:, None, :, None, :] * eye_b[None, :, None, :, None]).reshape(
            HEAD_DIM, bpp * CMP_BLOCK, bpp * hid)
        w1p = w1d.reshape(HEAD_DIM // 2, 2 * bpp * CMP_BLOCK, bpp * hid).astype(BF16)
        w2b = (w2[None, :, None, :] * eye_b[:, None, :, None]).reshape(bpp * hid, bpp * HEAD_DIM).astype(BF16)
        return jnp.tile(pe.T, (1, bpp)), w1p, w2b

    return (w_re, big(pe_k, w1_k, w2_k), big(pe_v, w1_v, w2_v), stored(pe_k, w1_k, w2_k),
            stored(pe_v, w1_v, w2_v))


def _nsa_prompt(x, nw, w_re, cmp_k, cmp_v, w_out, slopes, batch, final_w):
    t = x.shape[0]
    n = t // batch
    proj = norm_matmul(x, nw, w_re)
    kv = [proj[:, 2048 + NSA_KV_W * i:2048 + NSA_KV_W * (i + 1)] for i in range(6)]
    ck, cv, sk, sv, wk, wv = kv
    row_w = CMP_BLOCK * NSA_KV_W
    bpp = PAGE_SIZE // CMP_BLOCK
    n_pages = t // PAGE_SIZE
    pt = jnp.arange(n_pages, dtype=jnp.int32).reshape(SEQ_PER_BATCH, n_pages // SEQ_PER_BATCH)
    nb = n // CMP_BLOCK
    comp = lambda rows, wts: compress_paged(rows.reshape(n_pages, bpp, row_w), pt, *wts).reshape(
        batch, nb, NSA_KV_HEADS, HEAD_DIM).transpose(0, 2, 1, 3).astype(BF16)
    o_cmp, mask = nsa_cmp_prompt(proj, comp(ck, cmp_k), comp(cv, cmp_v), slopes, batch)
    slope_tab, pos_tab, ones_tab = _flash_tables(slopes, n)
    blk = NSA_KV_BLK0
    o_sel = flash_prompt(proj, slope_tab, pack_heads(proj, blk + 2, 1, batch, pos_tab, _block_onehot(n, CMP_BLOCK)),
                         pack_heads(proj, blk + 3, 1, batch, ones_tab), batch, mask=mask, tq=128, tk=1024)
    o_win = flash_prompt(proj, slope_tab, pack_heads(proj, blk + 4, 1, batch, pos_tab),
                         pack_heads(proj, blk + 5, 1, batch, ones_tab), batch, window=WINDOW, tq=256, tk=256)
    y = out_proj(x, proj, NSA_Z_BLK, [o_cmp, o_sel, o_win], w_out, gl_blk=NSA_GL_BLK, final_w=final_w)
    st = lambda a: a.reshape(batch, n, NSA_KV_HEADS, HEAD_DIM)
    keep = min(WINDOW, n)
    return y, (st(ck), st(cv), st(sk), st(sv), st(wk)[:, n - keep:], st(wv)[:, n - keep:])


def _nsa_sample(x, nw, w_re, cmp_k, cmp_v, w_out, slopes, n_seq, final_w, page_table, layer_j, caches):
    ck_t, cv_t, sk_t, sv_t, wk_t, wv_t, win_k, win_v, n_pool = caches
    n_tok = x.shape[0] // n_seq
    n_pages = page_table.shape[1]
    past = n_pages * PAGE_SIZE
    win_len = win_k.shape[2]
    bpp = PAGE_SIZE // CMP_BLOCK
    proj = norm_matmul(x, nw, w_re)
    kv = [proj[:, 2048 + NSA_KV_W * i:2048 + NSA_KV_W * (i + 1)] for i in range(6)]
    ck, cv, sk, sv, wk, wv = kv
    pt_layer = page_table + layer_j * n_pool
    comp = lambda pool_t, wts: compress_native(pool_t, pt_layer, *wts).reshape(
        n_seq, n_pages, NSA_KV_HEADS, bpp, HEAD_DIM).transpose(0, 1, 3, 2, 4).reshape(
        n_seq, n_pages * bpp, NSA_KV_W)
    col_slope, col_ints = _column_tables(slopes, past, n_tok)
    qbd = _expand_query(proj[:, :1024], n_seq, n_tok, NSA_GROUP)
    o_cmp, mask = nsa_sample_cmp(qbd, comp(ck_t, cmp_k), comp(cv_t, cmp_v), col_slope, col_ints)
    q_rows = qbd.transpose(0, 2, 1)
    row_slope, row_pos = _row_tables(col_slope, col_ints)
    new_t = lambda a: _new_rows_t(a, n_seq, n_tok)
    block_onehot = (jnp.arange(past // CMP_BLOCK)[:, None] == jnp.arange(past)[None, :] // CMP_BLOCK).astype(BF16)
    o_sel = sample_attn("sel", q_rows, sk_t, sv_t, page_table, new_t(sk), new_t(sv), row_slope, row_pos, 0, n_tok,
                        page0=layer_j * n_pool, mask=mask, block_onehot=block_onehot)
    assert win_len == WINDOW, "the updated window keeps exactly the buffer's length"
    o_win, kw_t, vw_t = sample_attn("win", q_rows, wk_t, wv_t, page_table, new_t(wk), new_t(wv), row_slope, row_pos,
                                    past - win_len, n_tok, page0=layer_j * n_seq, win_chunks=win_len // PAGE_SIZE)
    own = lambda o: _take_own(o, n_seq, n_tok, NSA_GROUP)
    y = out_proj(x, proj, NSA_Z_BLK, [own(o_cmp), own(o_sel), own(o_win)], w_out, gl_blk=NSA_GL_BLK,
                 final_w=final_w)
    st = lambda a: a.reshape(n_seq, n_tok, NSA_KV_HEADS, HEAD_DIM)
    tokens_major = lambda a_t: a_t.reshape(n_seq, NSA_KV_HEADS, HEAD_DIM, win_len).transpose(0, 3, 1, 2)
    return y, (st(ck), st(cv), st(sk), st(sv), tokens_major(kw_t), tokens_major(vw_t))


def _nsa_cache_views(cmp_k, cmp_v, sel_k, sel_v, win_k, win_v):
    n_pool = cmp_k.shape[1]
    return (_stored_tiles(cmp_k), _stored_tiles(cmp_v), _stored_tiles(sel_k), _stored_tiles(sel_v),
            _stored_tiles(win_k), _stored_tiles(win_v), win_k, win_v, n_pool)


def _moba_prompt(x, nw, w_in, w_out, slopes, batch, final_w):
    t = x.shape[0]
    n = t // batch
    proj = norm_matmul(x, nw, w_in)
    k, v = proj[:, 1024:2048], proj[:, 2048:3072]
    nf = n // MOBA_BLOCK
    kmean = block_mean(proj, 1, MOBA_BLOCK).reshape(batch, nf, N_HEADS, HEAD_DIM).transpose(0, 2, 1, 3).astype(BF16)
    mask = moba_select(proj, kmean, batch)
    slope_tab, pos_tab, ones_tab = _flash_tables(slopes, n)
    o = flash_prompt(proj, slope_tab, pack_heads(proj, 4, 4, batch, pos_tab, _block_onehot(n, MOBA_BLOCK)),
                     pack_heads(proj, 8, 4, batch, ones_tab), batch, mask=mask, tq=512, tk=1024)
    y = out_proj(x, proj, 3, [o], w_out, final_w=final_w)
    st = lambda a: a.reshape(batch, n, N_HEADS, HEAD_DIM)
    return y, (st(k), st(v))


def _moba_sample(x, nw, w_in, w_out, slopes, n_seq, final_w, page_table, k_pool, v_pool):
    n_tok = x.shape[0] // n_seq
    past = page_table.shape[1] * PAGE_SIZE
    assert past % MOBA_BLOCK == 0 and n_tok <= MOBA_BLOCK
    proj = norm_matmul(x, nw, w_in)
    k, v = proj[:, 1024:2048], proj[:, 2048:3072]
    q_rows = _expand_query(proj[:, :1024], n_seq, n_tok, 1).transpose(0, 2, 1)
    row_slope, row_pos = _row_tables(*_column_tables(slopes, past, n_tok))
    o = sample_attn("moba", q_rows, _stored_tiles(k_pool), _stored_tiles(v_pool), page_table,
                    _new_rows_t(k, n_seq, n_tok), _new_rows_t(v, n_seq, n_tok), row_slope, row_pos, 0, n_tok)
    y = out_proj(x, proj, 3, [_take_own(o, n_seq, n_tok, 1)], w_out, final_w=final_w)
    st = lambda a: a.reshape(n_seq, n_tok, N_HEADS, HEAD_DIM)
    return y, (st(k), st(v))


def _ret_prompt_layer(x, nw, w_in, gn_w, gn_b, w_out, batch, final_w):
    proj = norm_matmul(x, nw, w_in)
    on, state = ret_prompt(proj, batch, gn_w, gn_b)
    return out_proj(x, proj, 3, [on], w_out, final_w=final_w), state


def _ret_sample_layer(x, nw, w_in, gn_w, gn_b, w_out, n_seq, final_w, state):
    n_tok = x.shape[0] // n_seq
    proj = norm_matmul(x, nw, w_in)
    on, new_state = ret_sample(_pad_rows(proj, n_seq, n_tok), state, gn_w, gn_b, n_tok)
    on = on[:, :n_tok].reshape(n_seq * n_tok, -1)
    return out_proj(x, proj, 3, [on], w_out, final_w=final_w), new_state


def kernel(x_prompt, x_sample, cache_nsa_cmp_k, cache_nsa_cmp_v, cache_nsa_sel_k, cache_nsa_sel_v,
           cache_nsa_win_k, cache_nsa_win_v, cache_moba_k, cache_moba_v, state_ret, page_table,
           norm_w, final_norm_w, nsa_w_in, nsa_pe_k, nsa_pe_v, nsa_w1_k, nsa_w2_k, nsa_w1_v, nsa_w2_v,
           nsa_w_out, moba_w_in, moba_w_out, ret_w_in, ret_gn_w, ret_gn_b, ret_w_out):
    batch, seq, d = x_prompt.shape
    n_seq, n_tok, _ = x_sample.shape
    depth = norm_w.shape[0]
    slopes = jnp.exp2(-8.0 * (jnp.arange(N_HEADS, dtype=F32) + 1.0) / N_HEADS)
    xp = x_prompt.reshape(batch * seq, d)
    xs = x_sample.reshape(n_seq * n_tok, d)
    nsa_p, nsa_s, moba_p, moba_s, ret_p, ret_s = [], [], [], [], [], []
    nsa_caches = _nsa_cache_views(cache_nsa_cmp_k, cache_nsa_cmp_v, cache_nsa_sel_k, cache_nsa_sel_v,
                                  cache_nsa_win_k, cache_nsa_win_v)
    for layer in range(depth):
        j = layer // N_MIXERS
        fw = final_norm_w if layer == depth - 1 else None
        nw = norm_w[layer]
        if layer % N_MIXERS == 0:
            w_re, cmp_k, cmp_v, cmp_k_st, cmp_v_st = _nsa_weights(
                nsa_w_in[j], nsa_pe_k[j], nsa_pe_v[j], nsa_w1_k[j], nsa_w2_k[j], nsa_w1_v[j], nsa_w2_v[j])
            w_out = nsa_w_out[j].astype(BF16)
            xp, stp = _nsa_prompt(xp, nw, w_re, cmp_k, cmp_v, w_out, slopes, batch, fw)
            xs, sts = _nsa_sample(xs, nw, w_re, cmp_k_st, cmp_v_st, w_out, slopes, n_seq, fw, page_table, j,
                                  nsa_caches)
            nsa_p.append(stp)
            nsa_s.append(sts)
        elif layer % N_MIXERS == 1:
            w_in, w_out = moba_w_in[j].astype(BF16), moba_w_out[j].astype(BF16)
            xp, stp = _moba_prompt(xp, nw, w_in, w_out, slopes, batch, fw)
            xs, sts = _moba_sample(xs, nw, w_in, w_out, slopes, n_seq, fw, page_table, cache_moba_k[j],
                                   cache_moba_v[j])
            moba_p.append(stp)
            moba_s.append(sts)
        else:
            w_in, w_out = ret_w_in[j].astype(BF16), ret_w_out[j].astype(BF16)
            xp, stp = _ret_prompt_layer(xp, nw, w_in, ret_gn_w[j], ret_gn_b[j], w_out, batch, fw)
            xs, sts = _ret_sample_layer(xs, nw, w_in, ret_gn_w[j], ret_gn_b[j], w_out, n_seq, fw, state_ret[j])
            ret_p.append(stp)
            ret_s.append(sts)
    st = lambda items, i: jnp.stack([s[i] for s in items])
    return (xp.reshape(batch, seq, d), xs.reshape(n_seq, n_tok, d),
            st(nsa_p, 0), st(nsa_p, 1), st(nsa_p, 2), st(nsa_p, 3), st(nsa_p, 4), st(nsa_p, 5),
            st(moba_p, 0), st(moba_p, 1), jnp.stack(ret_p),
            st(nsa_s, 0), st(nsa_s, 1), st(nsa_s, 2), st(nsa_s, 3), st(nsa_s, 4), st(nsa_s, 5),
            st(moba_s, 0), st(moba_s, 1), jnp.stack(ret_s))
```

```python
import functools

import jax
import jax.numpy as jnp
import numpy as np
from jax import lax
from jax.experimental import pallas as pl
from jax.experimental.pallas import tpu as pltpu

F32 = jnp.float32
BF16 = jnp.bfloat16

HEAD_DIM = 64
N_HEADS = 16
NSA_KV_HEADS = 4
NSA_GROUP = 4
CMP_BLOCK = 64
SEL_TOP = 15
WINDOW = 512
MOBA_BLOCK = 256
MOBA_TOP = 3
RET_HEADS = 4
RET_DK = 256
RET_CHUNK = 128
PAGE_SIZE = 128
RMS_EPS = 1e-6
GN_EPS = 1e-5
NEG_INF = -1e30
ATTN_SCALE = HEAD_DIM ** -0.5
N_MIXERS = 3

VMEM_LIMIT = 56 * 1024 * 1024


def _cparams(*sem):
    return pltpu.CompilerParams(dimension_semantics=sem, vmem_limit_bytes=VMEM_LIMIT)


def _nt(a, b):
    return lax.dot_general(a, b, (((1,), (1,)), ((), ())), preferred_element_type=F32)


def _nn(a, b):
    return lax.dot_general(a, b, (((1,), (0,)), ((), ())), preferred_element_type=F32)


def _tn(a, b):
    return lax.dot_general(a, b, (((0,), (0,)), ((), ())), preferred_element_type=F32)


def _sigmoid(x):
    return 1.0 / (1.0 + jnp.exp(-x))


def _norm_matmul_kernel(x_ref, nw_ref, w_ref, o_ref, xn_ref):
    @pl.when(pl.program_id(1) == 0)
    def _():
        x = x_ref[...]
        ms = jnp.mean(x * x, axis=-1, keepdims=True)
        xn_ref[...] = ((x * lax.rsqrt(ms + RMS_EPS)) * nw_ref[...]).astype(BF16)

    o_ref[...] = _nn(xn_ref[...], w_ref[...])


def norm_matmul(x, nw, w_bf16):
    t, d = x.shape
    n = w_bf16.shape[1]
    tm = min(t, 1024)
    tn = 1024 if n % 1024 == 0 else 768
    assert n % tn == 0
    return pl.pallas_call(
        _norm_matmul_kernel,
        grid=(t // tm, n // tn),
        in_specs=[pl.BlockSpec((tm, d), lambda i, j: (i, 0)),
                  pl.BlockSpec((1, d), lambda i, j: (0, 0)),
                  pl.BlockSpec((d, tn), lambda i, j: (0, j))],
        out_specs=pl.BlockSpec((tm, tn), lambda i, j: (i, j)),
        out_shape=jax.ShapeDtypeStruct((t, n), F32),
        scratch_shapes=[pltpu.VMEM((tm, d), BF16)],
        compiler_params=_cparams("parallel", "arbitrary"),
        name="norm_matmul",
    )(x, nw.reshape(1, d), w_bf16)


def _out_proj_kernel(*refs, n_o, nsa_gates, final_norm):
    x_ref, z_ref = refs[0], refs[1]
    o_refs = refs[2:2 + n_o]
    pos = 2 + n_o
    if nsa_gates:
        gl_ref = refs[pos]
        pos += 1
    w_ref = refs[pos]
    pos += 1
    if final_norm:
        fw_ref = refs[pos]
        pos += 1
    y_ref = refs[pos]

    if nsa_gates:
        gates = _sigmoid(gl_ref[...])
        tm = gates.shape[0]
        lane = lax.broadcasted_iota(jnp.int32, (tm, 128), 1)
        cols = []
        for vb in range(N_HEADS // 2):
            acc = None
            for br in range(n_o):
                c0 = (2 * vb) * 3 + br
                c1 = (2 * vb + 1) * 3 + br
                g = jnp.where(lane < HEAD_DIM,
                              jnp.broadcast_to(gates[:, c0:c0 + 1], (tm, 128)),
                              jnp.broadcast_to(gates[:, c1:c1 + 1], (tm, 128)))
                term = g * o_refs[br][:, vb * 128:(vb + 1) * 128]
                acc = term if acc is None else acc + term
            cols.append(acc)
        o = jnp.concatenate(cols, axis=1)
    else:
        o = o_refs[0][...]
    z = z_ref[...]
    gated = (o * (z * _sigmoid(z))).astype(BF16)
    y = x_ref[...] + _nn(gated, w_ref[...])
    if final_norm:
        ms = jnp.mean(y * y, axis=-1, keepdims=True)
        y = (y * lax.rsqrt(ms + RMS_EPS)) * fw_ref[...]
    y_ref[...] = y


def out_proj(x, proj, z_blk, o_list, w_bf16, gl_blk=None, final_w=None, tm=256):
    t, d = x.shape
    tm = min(tm, t)
    n_o = len(o_list)
    in_specs = [pl.BlockSpec((tm, d), lambda i: (i, 0)),
                pl.BlockSpec((tm, d), lambda i: (i, z_blk))]
    args = [x, proj]
    for o in o_list:
        in_specs.append(pl.BlockSpec((tm, d), lambda i: (i, 0)))
        args.append(o)
    if gl_blk is not None:
        in_specs.append(pl.BlockSpec((tm, 256), lambda i: (i, gl_blk)))
        args.append(proj)
    in_specs.append(pl.BlockSpec((d, d), lambda i: (0, 0)))
    args.append(w_bf16)
    if final_w is not None:
        in_specs.append(pl.BlockSpec((1, d), lambda i: (0, 0)))
        args.append(final_w.reshape(1, d))
    kern = functools.partial(_out_proj_kernel, n_o=n_o, nsa_gates=gl_blk is not None,
                             final_norm=final_w is not None)
    return pl.pallas_call(
        kern,
        grid=(t // tm,),
        in_specs=in_specs,
        out_specs=pl.BlockSpec((tm, d), lambda i: (i, 0)),
        out_shape=jax.ShapeDtypeStruct((t, d), F32),
        compiler_params=_cparams("parallel"),
        name="out_proj",
    )(*args)


def _ret_tables(c_true, c_pad):
    lg = jnp.log1p(-jnp.exp2(-5.0 - jnp.arange(RET_HEADS, dtype=F32)))[:, None, None]
    i = jnp.arange(c_pad, dtype=F32)
    live = (i < c_true)
    diff = i[:, None] - i[None, :]
    dmat = jnp.where((diff >= 0) & live[:, None] & live[None, :], jnp.exp(lg * jnp.maximum(diff, 0.0)), 0.0)
    qdec = jnp.exp(lg * (i[:, None] + 1.0)) * jnp.ones((1, 1, 128), F32)
    kdec = jnp.where(live[:, None], jnp.exp(lg * (c_true - 1.0 - i[:, None])), 0.0) * jnp.ones((1, 1, 128), F32)
    sdec = jnp.exp(lg * c_true) * jnp.ones((1, 8, 128), F32)
    return dmat, qdec, kdec, sdec


def _ret_chunk(q, k, v, state, dmat, qdec, kdec, sdec, gw, gb):
    k = k * (RET_DK ** -0.5)
    qb, kb, vb = q.astype(BF16), k.astype(BF16), v.astype(BF16)
    inner = _nt(qb, kb) * dmat
    o = _nn(inner.astype(BF16), vb) + _nn(qb, state.astype(BF16)) * qdec
    kw = (k * kdec).astype(BF16)
    new_state = sdec * state + _nn(kw.T, vb)

    mu = jnp.mean(o, axis=-1, keepdims=True)
    var = jnp.mean(jnp.square(o - mu), axis=-1, keepdims=True)
    on = (o - mu) * lax.rsqrt(var + GN_EPS)
    return on * gw + gb, new_state


def _ret_prompt_kernel(q_ref, k_ref, v_ref, d_ref, qd_ref, kd_ref, sd_ref, gw_ref, gb_ref,
                       o_ref, sn_ref, st_ref, *, n_chunks):
    c = pl.program_id(2)

    @pl.when(c == 0)
    def _():
        st_ref[...] = jnp.zeros_like(st_ref)

    on, new_state = _ret_chunk(q_ref[...], k_ref[...], v_ref[...], st_ref[...], d_ref[0], qd_ref[0][:, 0:1],
                               kd_ref[0][:, 0:1], sd_ref[0][0:1, 0:1], gw_ref[...], gb_ref[...])
    o_ref[...] = on
    st_ref[...] = new_state

    @pl.when(c == n_chunks - 1)
    def _():
        sn_ref[0, 0] = new_state


def _ret_sample_kernel(x_ref, s0_ref, d_ref, qd_ref, kd_ref, sd_ref, gw_ref, gb_ref, o_ref, sn_ref):
    w = RET_DK
    for h in range(RET_HEADS):
        cols = lambda blk: x_ref[0, :, (blk * RET_HEADS + h) * w:(blk * RET_HEADS + h + 1) * w]
        on, new_state = _ret_chunk(cols(0), cols(1), cols(2), s0_ref[0, h], d_ref[h], qd_ref[h][:, 0:1],
                                   kd_ref[h][:, 0:1], sd_ref[h][0:1, 0:1], gw_ref[:, h * w:(h + 1) * w],
                                   gb_ref[:, h * w:(h + 1) * w])
        o_ref[0, :, h * w:(h + 1) * w] = on
        sn_ref[0, h] = new_state


def ret_prompt(proj, batch, gn_w, gn_b):
    t = proj.shape[0]
    n = t // batch
    nc = n // RET_CHUNK
    c = RET_CHUNK
    dmat, qdec, kdec, sdec = _ret_tables(c, c)
    hh = RET_HEADS
    tab = lambda r: pl.BlockSpec((1, r, 128), lambda b, h, j: (h, 0, 0))
    return pl.pallas_call(
        functools.partial(_ret_prompt_kernel, n_chunks=nc),
        grid=(batch, hh, nc),
        in_specs=[pl.BlockSpec((c, 256), lambda b, h, j: (b * nc + j, h)),
                  pl.BlockSpec((c, 256), lambda b, h, j: (b * nc + j, hh + h)),
                  pl.BlockSpec((c, 256), lambda b, h, j: (b * nc + j, 2 * hh + h)),
                  pl.BlockSpec((1, c, c), lambda b, h, j: (h, 0, 0)),
                  tab(c), tab(c), tab(8),
                  pl.BlockSpec((1, 256), lambda b, h, j: (0, h)),
                  pl.BlockSpec((1, 256), lambda b, h, j: (0, h))],
        out_specs=[pl.BlockSpec((c, 256), lambda b, h, j: (b * nc + j, h)),
                   pl.BlockSpec((1, 1, RET_DK, 256), lambda b, h, j: (b, h, 0, 0))],
        out_shape=[jax.ShapeDtypeStruct((t, 1024), F32),
                   jax.ShapeDtypeStruct((batch, hh, RET_DK, 256), F32)],
        scratch_shapes=[pltpu.VMEM((RET_DK, 256), F32)],
        compiler_params=_cparams("parallel", "parallel", "arbitrary"),
        name="ret_prompt",
    )(proj, proj, proj, dmat, qdec, kdec, sdec, gn_w.reshape(1, -1), gn_b.reshape(1, -1))


def ret_sample(proj3, state, gn_w, gn_b, n_tok):
    b, cp = proj3.shape[0], proj3.shape[1]
    dmat, qdec, kdec, sdec = _ret_tables(n_tok, cp)
    hh = RET_HEADS
    whole = lambda a: pl.BlockSpec(a.shape, lambda s: (0,) * a.ndim)
    gw, gb = gn_w.reshape(1, -1), gn_b.reshape(1, -1)
    return pl.pallas_call(
        _ret_sample_kernel,
        grid=(b,),
        in_specs=[pl.BlockSpec((1, cp, proj3.shape[2]), lambda s: (s, 0, 0)),
                  pl.BlockSpec((1, hh, RET_DK, 256), lambda s: (s, 0, 0, 0)),
                  whole(dmat), whole(qdec), whole(kdec), whole(sdec), whole(gw), whole(gb)],
        out_specs=[pl.BlockSpec((1, cp, hh * 256), lambda s: (s, 0, 0)),
                   pl.BlockSpec((1, hh, RET_DK, 256), lambda s: (s, 0, 0, 0))],
        out_shape=[jax.ShapeDtypeStruct((b, cp, hh * 256), F32),
                   jax.ShapeDtypeStruct((b, hh, RET_DK, 256), F32)],
        compiler_params=_cparams("parallel"),
        name="ret_sample",
    )(proj3, state, dmat, qdec, kdec, sdec, gw, gb)


HB = 4


LANES = 128


LOG2E = 1.4426950408889634
ALIBI_COLS = 6


def _low_half(ref, h):
    col = ref[:, (h // 2) * LANES:(h // 2 + 1) * LANES]
    return pltpu.roll(col, HEAD_DIM, 1) if h % 2 else col


def _flash_kernel(q_ref, tab_ref, k_ref, v_ref, *rest, tq, tk, n_stack, window, masked):
    if masked:
        mask_ref, o_ref, m_ref, acc_ref = rest
    else:
        o_ref, m_ref, acc_ref = rest
    g = pl.program_id(1)
    q0 = pl.program_id(2) * tq
    n_loop = HB // n_stack
    lane = lax.broadcasted_iota(jnp.int32, (1, LANES), 1)
    qpos = q0 + lax.broadcasted_iota(jnp.int32, (1, tq, 1), 1)
    lo = jnp.maximum(q0 - window + 1, 0) // tk if window else 0
    hi = (q0 + tq + tk - 1) // tk

    for i in range(n_loop):
        heads = [i * n_stack + r for r in range(n_stack)]
        q_st = jnp.concatenate(
            [jnp.where(lane < HEAD_DIM, _low_half(q_ref, h) * (ATTN_SCALE * LOG2E), tab_ref[pl.ds(g * HB + h, 1), :])
             for h in heads], axis=0).astype(BF16)
        if masked:
            q_st = jnp.concatenate([q_st, jnp.concatenate([mask_ref[0, i]] * n_stack, axis=0)], axis=1)
        m_ref[...] = jnp.full(m_ref.shape, NEG_INF, F32)
        acc_ref[...] = jnp.zeros(acc_ref.shape, F32)

        def tile(k0, width, edge):
            s = _nt(q_st, k_ref[0, i, pl.ds(k0, width), :])
            if edge:
                kpos = k0 + lax.broadcasted_iota(jnp.int32, (1, 1, width), 2)
                ok = kpos <= qpos
                if window:
                    ok = ok & ((qpos - kpos) < window)
                s = jnp.where(ok, s.reshape(n_stack, tq, width), NEG_INF).reshape(n_stack * tq, width)
            m_prev = m_ref[...]
            m_new = jnp.maximum(m_prev, jnp.max(s, axis=-1, keepdims=True))
            p = jnp.exp2(s - jnp.concatenate([m_new] * (width // LANES), axis=1))
            acc_ref[...] = (jnp.exp2(m_prev - m_new) * acc_ref[...]
                            + _nn(p.astype(BF16), v_ref[0, i, pl.ds(k0, width), :]))
            m_ref[...] = m_new

        def edge_body(j, carry):
            tile(pl.multiple_of(j * tk, tk), tk, True)
            return carry

        def inner_body(j, carry):
            tile(pl.multiple_of(j * tk, tk), tk, False)
            return carry

        if window:
            lax.fori_loop(lo, hi, edge_body, 0)
        else:
            lax.fori_loop(lo, hi - 1, inner_body, 0)
            tile(pl.multiple_of((hi - 1) * tk, tk), tk, True)
        acc = acc_ref[...]
        o = acc[:, :HEAD_DIM] / acc[:, HEAD_DIM:HEAD_DIM + 1]
        for r, h in enumerate(heads):
            o_ref[:, h * HEAD_DIM:(h + 1) * HEAD_DIM] = o[r * tq:(r + 1) * tq]


def flash_prompt(proj, slope_tab, k_aug, v_aug, batch, mask=None, window=0, tq=128, tk=512):
    seq = k_aug.shape[2]
    assert tk % tq == 0 or window, "the causal edge is taken to be the last key tile only"
    n_groups = N_HEADS // HB
    nq = seq // tq
    n_stack = HB if k_aug.shape[1] == n_groups else 1
    kvb = HB // n_stack
    rows = n_stack * tq
    masked = mask is not None
    in_specs = [pl.BlockSpec((tq, HB * HEAD_DIM), lambda b, g, i: (b * nq + i, g)),
                pl.BlockSpec(slope_tab.shape, lambda b, g, i: (0, 0)),
                pl.BlockSpec((1, kvb) + k_aug.shape[2:], lambda b, g, i: (b, g, 0, 0)),
                pl.BlockSpec((1, kvb, seq, LANES), lambda b, g, i: (b, g, 0, 0))]
    args = [proj, slope_tab, k_aug, v_aug]
    if masked:
        in_specs.append(pl.BlockSpec((1, kvb, tq, LANES), lambda b, g, i: (b, g, i, 0)))
        args.append(mask)
    kern = functools.partial(_flash_kernel, tq=tq, tk=tk, n_stack=n_stack, window=window, masked=masked)
    return pl.pallas_call(
        kern,
        grid=(batch, n_groups, nq),
        in_specs=in_specs,
        out_specs=pl.BlockSpec((tq, HB * HEAD_DIM), lambda b, g, i: (b * nq + i, g)),
        out_shape=jax.ShapeDtypeStruct((batch * seq, N_HEADS * HEAD_DIM), F32),
        scratch_shapes=[pltpu.VMEM((rows, LANES), F32), pltpu.VMEM((rows, LANES), F32)],
        compiler_params=_cparams("parallel", "parallel", "arbitrary"),
        name="flash_prompt",
    )(*args)


def _pack_kernel(x_ref, tab_ref, *rest, with_onehot):
    if with_onehot:
        oh_ref, o_ref = rest
    else:
        o_ref, = rest
    lane = lax.broadcasted_iota(jnp.int32, (1, LANES), 1)
    for h in range(HB):
        row = jnp.where(lane < HEAD_DIM, _low_half(x_ref, h), tab_ref[...]).astype(BF16)
        o_ref[0, h] = jnp.concatenate([row, oh_ref[...]], axis=1) if with_onehot else row


def pack_heads(proj, col_blk0, n_col_blks, batch, tab, onehot=None, ts=512):
    t = proj.shape[0]
    seq = t // batch
    ns = seq // ts
    with_onehot = onehot is not None
    width = 2 * LANES if with_onehot else LANES
    in_specs = [pl.BlockSpec((ts, HB * HEAD_DIM), lambda b, c, i: (b * ns + i, col_blk0 + c)),
                pl.BlockSpec((ts, LANES), lambda b, c, i: (i, 0))]
    args = [proj, tab]
    if with_onehot:
        in_specs.append(pl.BlockSpec((ts, LANES), lambda b, c, i: (i, 0)))
        args.append(onehot)
    return pl.pallas_call(
        functools.partial(_pack_kernel, with_onehot=with_onehot),
        grid=(batch, n_col_blks, ns),
        in_specs=in_specs,
        out_specs=pl.BlockSpec((1, HB, ts, width), lambda b, c, i: (b, c, i, 0)),
        out_shape=jax.ShapeDtypeStruct((batch, HB * n_col_blks, seq, width), BF16),
        compiler_params=_cparams("parallel", "parallel", "parallel"),
        name="pack_heads",
    )(*args)


def _rank_select(v, cand, blk, n_blocks, n_top):
    v = jnp.where(cand, v, -jnp.inf)
    rank = jnp.zeros(v.shape, jnp.int32)
    for i in range(n_blocks):
        vi = v[i:i + 1, :]
        ahead = (vi > v) | ((vi == v) & (blk > i))
        rank = rank + ahead.astype(jnp.int32)
    return cand & (rank < n_top)


def _moba_select_kernel(q_ref, km_ref, o_ref, *, tq, n_blocks):
    q0 = pl.program_id(2) * tq
    own = (q0 + lax.broadcasted_iota(jnp.int32, (1, tq), 1)) // MOBA_BLOCK
    blk = lax.broadcasted_iota(jnp.int32, (n_blocks, 1), 0)
    for r in range(HB):
        q = q_ref[:, r * HEAD_DIM:(r + 1) * HEAD_DIM].astype(BF16)
        score = _nt(km_ref[0, r], q)
        sel = _rank_select(score, blk < own, blk, n_blocks, MOBA_TOP) | (blk == own)
        o_ref[0, r] = _mask_rows(sel, n_blocks)


def _mask_rows(sel, n_blocks):
    eye = (lax.broadcasted_iota(jnp.int32, (n_blocks, 1), 0)
           == lax.broadcasted_iota(jnp.int32, (1, LANES), 1)).astype(BF16)
    kept = _tn(jnp.where(sel, 1.0, 0.0).astype(BF16), eye)
    lane = lax.broadcasted_iota(jnp.int32, (1, LANES), 1)
    return jnp.where((lane < n_blocks) & (kept < 0.5), NEG_INF, 0.0).astype(BF16)


def moba_select(proj, kmean, batch, tq=256):
    nb = kmean.shape[2]
    seq = proj.shape[0] // batch
    nq = seq // tq
    return pl.pallas_call(
        functools.partial(_moba_select_kernel, tq=tq, n_blocks=nb),
        grid=(batch, N_HEADS // HB, nq),
        in_specs=[pl.BlockSpec((tq, HB * HEAD_DIM), lambda b, g, i: (b * nq + i, g)),
                  pl.BlockSpec((1, HB, nb, HEAD_DIM), lambda b, g, i: (b, g, 0, 0))],
        out_specs=pl.BlockSpec((1, HB, tq, LANES), lambda b, g, i: (b, g, i, 0)),
        out_shape=jax.ShapeDtypeStruct((batch, N_HEADS, seq, LANES), BF16),
        compiler_params=_cparams("parallel", "parallel", "parallel"),
        name="moba_select",
    )(proj, kmean)


def _block_mean_kernel(k_ref, o_ref):
    o_ref[0] = jnp.mean(k_ref[...], axis=0, keepdims=True)


def block_mean(proj, col_blk, rows):
    t = proj.shape[0]
    return pl.pallas_call(
        _block_mean_kernel,
        grid=(t // rows,),
        in_specs=[pl.BlockSpec((rows, 1024), lambda i: (i, col_blk))],
        out_specs=pl.BlockSpec((1, 1, 1024), lambda i: (i, 0, 0)),
        out_shape=jax.ShapeDtypeStruct((t // rows, 1, 1024), F32),
        compiler_params=_cparams("parallel"),
        name="block_mean",
    )(proj)


def _nsa_cmp_kernel(slopes_ref, q_ref, kc_ref, vc_ref, o_ref, sel_ref, *, tq, n_blocks):
    g = pl.program_id(1)
    q0 = pl.program_id(2) * tq
    kc = kc_ref[0, 0]
    vc = vc_ref[0, 0]
    qpos_l = q0 + lax.broadcasted_iota(jnp.int32, (1, tq), 1)
    blk_s = lax.broadcasted_iota(jnp.int32, (n_blocks, 1), 0)
    own_l = qpos_l // CMP_BLOCK
    valid_t = blk_s < own_l
    dist_t = (qpos_l - (blk_s * CMP_BLOCK + (CMP_BLOCK - 1))).astype(F32)
    imp_t = jnp.zeros((n_blocks, tq), F32)
    for r in range(HB):
        slope = slopes_ref[g * HB + r]
        q = (q_ref[:, r * HEAD_DIM:(r + 1) * HEAD_DIM] * ATTN_SCALE).astype(BF16)
        s_t = jnp.where(valid_t, _nt(kc, q) - slope * dist_t, NEG_INF)
        e_t = jnp.where(valid_t, jnp.exp(s_t - jnp.max(s_t, axis=0, keepdims=True)), 0.0)
        p_t = e_t / jnp.maximum(jnp.sum(e_t, axis=0, keepdims=True), 1e-30)
        imp_t = imp_t + p_t
        o_ref[:, r * HEAD_DIM:(r + 1) * HEAD_DIM] = _tn(p_t.astype(BF16), vc)
    sel = _rank_select(imp_t, valid_t, blk_s, n_blocks, SEL_TOP) | (blk_s == own_l)
    sel_ref[0, 0] = _mask_rows(sel, n_blocks)


def nsa_cmp_prompt(proj, kc, vc, slopes, batch, tq=256):
    nb = kc.shape[2]
    seq = proj.shape[0] // batch
    nq = seq // tq
    return pl.pallas_call(
        functools.partial(_nsa_cmp_kernel, tq=tq, n_blocks=nb),
        grid_spec=pltpu.PrefetchScalarGridSpec(
            num_scalar_prefetch=1,
            grid=(batch, NSA_KV_HEADS, nq),
            in_specs=[pl.BlockSpec((tq, HB * HEAD_DIM), lambda b, g, i, s: (b * nq + i, g)),
                      pl.BlockSpec((1, 1, nb, HEAD_DIM), lambda b, g, i, s: (b, g, 0, 0)),
                      pl.BlockSpec((1, 1, nb, HEAD_DIM), lambda b, g, i, s: (b, g, 0, 0))],
            out_specs=[pl.BlockSpec((tq, HB * HEAD_DIM), lambda b, g, i, s: (b * nq + i, g)),
                       pl.BlockSpec((1, 1, tq, LANES), lambda b, g, i, s: (b, g, i, 0))]),
        out_shape=[jax.ShapeDtypeStruct((batch * seq, N_HEADS * HEAD_DIM), F32),
                   jax.ShapeDtypeStruct((batch, NSA_KV_HEADS, seq, LANES), BF16)],
        compiler_params=_cparams("parallel", "parallel", "parallel"),
        name="nsa_cmp_prompt",
    )(slopes, proj, kc, vc)


SEQ_PER_BATCH = 8
SEQ_PER_STEP = 4


def _compress_kernel(pt_ref, *refs, n_pages, kchunk):
    page_refs = refs[:n_pages]
    pe_ref, w1_ref, w2_ref, o_ref, stage_ref = refs[n_pages:]
    s = pl.program_id(0)
    slot = s % SEQ_PER_BATCH
    bpp = page_refs[0].shape[1]
    for p in range(n_pages):
        stage_ref[slot, p * bpp:(p + 1) * bpp, :] = page_refs[p][0]

    @pl.when(slot == SEQ_PER_BATCH - 1)
    def _():
        rows = SEQ_PER_BATCH * n_pages * bpp
        width = stage_ref.shape[-1]
        hid = jnp.zeros((rows, w1_ref.shape[1]), F32)
        for c in range(width // kchunk):
            x = stage_ref[:, :, c * kchunk:(c + 1) * kchunk].reshape(rows, kchunk)
            x = (x + pe_ref[:, c * kchunk:(c + 1) * kchunk]).astype(BF16)
            hid = hid + _nn(x, w1_ref[c * kchunk:(c + 1) * kchunk, :])
        act = (hid * _sigmoid(hid)).astype(BF16)
        o_ref[...] = _nn(act, w2_ref[...])


def compress_paged(pool2, page_table, pe_big, w1_big, w2_big):
    n_seq, n_pages = page_table.shape
    bpp, width = pool2.shape[1], pool2.shape[2]
    rows = SEQ_PER_BATCH * n_pages * bpp
    gw = w1_big.shape[1]
    in_specs = [pl.BlockSpec((1, bpp, width), functools.partial(lambda s, pt, p: (pt[s, p], 0, 0), p=p))
                for p in range(n_pages)]
    in_specs += [pl.BlockSpec((1, width), lambda s, pt: (0, 0)),
                 pl.BlockSpec((width, gw), lambda s, pt: (0, 0)),
                 pl.BlockSpec((gw, gw), lambda s, pt: (0, 0))]
    return pl.pallas_call(
        functools.partial(_compress_kernel, n_pages=n_pages, kchunk=2048),
        grid_spec=pltpu.PrefetchScalarGridSpec(
            num_scalar_prefetch=1,
            grid=(n_seq,),
            in_specs=in_specs,
            out_specs=pl.BlockSpec((rows, gw), lambda s, pt: (s // SEQ_PER_BATCH, 0)),
            scratch_shapes=[pltpu.VMEM((SEQ_PER_BATCH, n_pages * bpp, width), F32)]),
        out_shape=jax.ShapeDtypeStruct((n_seq * n_pages * bpp, gw), F32),
        compiler_params=_cparams("arbitrary"),
        name="compress_paged",
    )(page_table, *([pool2] * n_pages), pe_big, w1_big, w2_big)


COLS = 128


def _nsa_sample_cmp_kernel(qbd_ref, kc_ref, vc_ref, cs_ref, cp_ref, o_ref, mask_ref, *, n_blocks):
    qbd = qbd_ref[0]
    kc = kc_ref[0].astype(BF16)
    vc = vc_ref[0].astype(BF16)
    slope = cs_ref[0:1, :]
    qpos = cp_ref[0:1, :]
    blk = lax.broadcasted_iota(jnp.int32, (n_blocks, 1), 0)
    valid = blk < qpos // CMP_BLOCK
    dist = (qpos - (blk * CMP_BLOCK + (CMP_BLOCK - 1))).astype(F32)
    s = jnp.where(valid, _nn(kc, qbd) - slope * dist, NEG_INF)
    e = jnp.where(valid, jnp.exp(s - jnp.max(s, axis=0, keepdims=True)), 0.0)
    p = e / jnp.maximum(jnp.sum(e, axis=0, keepdims=True), 1e-30)
    o_ref[0] = _tn(p.astype(BF16), vc)
    tot = p + pltpu.roll(p, 4, 1) + pltpu.roll(p, 8, 1) + pltpu.roll(p, 12, 1)
    col = lax.broadcasted_iota(jnp.int32, (1, COLS), 1)
    tot = jnp.where(col % 16 >= 12, tot, 0.0)
    imp = tot + pltpu.roll(tot, COLS - 4, 1) + pltpu.roll(tot, COLS - 8, 1) + pltpu.roll(tot, COLS - 12, 1)
    sel = _rank_select(imp, valid, blk, n_blocks, SEL_TOP)
    mask_ref[0] = jnp.where(sel, 0.0, NEG_INF).astype(BF16)


def nsa_sample_cmp(qbd, kc, vc, col_slope, col_pos):
    n_seq, nb, c = kc.shape
    return pl.pallas_call(
        functools.partial(_nsa_sample_cmp_kernel, n_blocks=nb),
        grid=(n_seq,),
        in_specs=[pl.BlockSpec((1, c, COLS), lambda s: (s, 0, 0)),
                  pl.BlockSpec((1, nb, c), lambda s: (s, 0, 0)),
                  pl.BlockSpec((1, nb, c), lambda s: (s, 0, 0)),
                  pl.BlockSpec((8, COLS), lambda s: (0, 0)),
                  pl.BlockSpec((8, COLS), lambda s: (0, 0))],
        out_specs=[pl.BlockSpec((1, COLS, c), lambda s: (s, 0, 0)),
                   pl.BlockSpec((1, nb, COLS), lambda s: (s, 0, 0))],
        out_shape=[jax.ShapeDtypeStruct((n_seq, COLS, c), F32),
                   jax.ShapeDtypeStruct((n_seq, nb, COLS), BF16)],
        compiler_params=_cparams("parallel"),
        name="nsa_sample_cmp",
    )(qbd, kc, vc, col_slope, col_pos)


def _rank_select_lanes(v, cand, n_blocks, n_top):
    lane = lax.broadcasted_iota(jnp.int32, (1, v.shape[1]), 1)
    v = jnp.where(cand, v, -jnp.inf)
    rank = jnp.zeros(v.shape, jnp.int32)
    for i in range(n_blocks):
        vi = v[:, i:i + 1]
        ahead = (vi > v) | ((vi == v) & (lane > i))
        rank = rank + ahead.astype(jnp.int32)
    return cand & (rank < n_top)


def _sample_attn_kernel(pt_ref, q_ref, *refs, mode, n_chunks, kpos0, n_new):
    kt_refs = refs[:n_chunks]
    vt_refs = refs[n_chunks:2 * n_chunks]
    knew_ref, vnew_ref, rs_ref, rp_ref = refs[2 * n_chunks:2 * n_chunks + 4]
    pos = 2 * n_chunks + 4
    if mode == "sel":
        mask_ref, e_ref = refs[pos], refs[pos + 1]
        pos += 2
    o_ref = refs[pos]
    if mode == "win":
        kw_ref, vw_ref = refs[pos + 1], refs[pos + 2]
    st_ref = refs[-1]
    del pt_ref
    q = q_ref[0]
    slope = rs_ref[...]
    qpos = rp_ref[...]
    lane = lax.broadcasted_iota(jnp.int32, (1, PAGE_SIZE), 1)
    n_past = n_chunks * PAGE_SIZE

    if mode == "sel":
        sel_bias = _tn(mask_ref[0], e_ref[...])
    if mode == "moba":
        per_blk = MOBA_BLOCK // PAGE_SIZE
        n_blk = n_chunks // per_blk
        kmean = jnp.zeros((q.shape[1], PAGE_SIZE), F32)
        for j in range(n_blk):
            tot = kt_refs[j * per_blk][0]
            for i in range(1, per_blk):
                tot = tot + kt_refs[j * per_blk + i][0]
            mean_j = jnp.sum(tot, axis=1, keepdims=True) * (1.0 / MOBA_BLOCK)
            kmean = jnp.where(lane == j, mean_j, kmean)
        keep = _rank_select_lanes(_nn(q, kmean.astype(BF16)), lane < qpos // MOBA_BLOCK, n_blk, MOBA_TOP)
        moba_bias = jnp.where(keep, 0.0, NEG_INF)

    for p in range(n_chunks):
        kpos = kpos0 + p * PAGE_SIZE + lane
        s = _nn(q, kt_refs[p][0].astype(BF16)) - slope * (qpos - kpos).astype(F32)
        if mode == "sel":
            s = s + sel_bias[:, p * PAGE_SIZE:(p + 1) * PAGE_SIZE]
        elif mode == "moba":
            s = s + moba_bias[:, p // per_blk:p // per_blk + 1]
        else:
            diff = qpos - kpos
            s = jnp.where((kpos >= 0) & (diff >= 0) & (diff < WINDOW), s, NEG_INF)
        st_ref[:, p * PAGE_SIZE:(p + 1) * PAGE_SIZE] = s
    kpos_new = kpos0 + n_past + lane
    s_new = _nn(q, knew_ref[0].astype(BF16)) - slope * (qpos - kpos_new).astype(F32)
    st_ref[:, n_past:] = jnp.where((lane < n_new) & (kpos_new <= qpos), s_new, NEG_INF)

    s_all = st_ref[...]
    e = jnp.exp(s_all - jnp.max(s_all, axis=1, keepdims=True))
    p_all = (e / jnp.sum(e, axis=1, keepdims=True)).astype(BF16)
    o = _nt(p_all[:, n_past:], vnew_ref[0].astype(BF16))
    for p in range(n_chunks):
        o = o + _nt(p_all[:, p * PAGE_SIZE:(p + 1) * PAGE_SIZE], vt_refs[p][0].astype(BF16))
    o_ref[0] = o
    if mode == "win":
        for src_refs, new_ref, dst_ref in ((kt_refs, knew_ref, kw_ref), (vt_refs, vnew_ref, vw_ref)):
            both = jnp.concatenate([r[0] for r in src_refs] + [new_ref[0]], axis=1)
            dst_ref[0] = both[:, n_new:n_new + n_past]


def sample_attn(mode, q_rows, kt_src, vt_src, page_table, kt_new, vt_new, row_slope, row_pos, kpos0, n_new,
                page0=0, win_chunks=0, mask=None, block_onehot=None):
    n_seq, rows, c = q_rows.shape
    if mode == "win":
        n_chunks = win_chunks
        src_spec = [pl.BlockSpec((1, c, PAGE_SIZE), functools.partial(lambda s, pt, p: (page0 + s, 0, p), p=p))
                    for p in range(n_chunks)]
    else:
        n_chunks = page_table.shape[1]
        src_spec = [pl.BlockSpec((1, c, PAGE_SIZE), functools.partial(
            lambda s, pt, p: (page0 + pt[s, p], 0, 0), p=p)) for p in range(n_chunks)]
    in_specs = [pl.BlockSpec((1, rows, c), lambda s, pt: (s, 0, 0))] + src_spec + src_spec
    in_specs += [pl.BlockSpec((1, c, PAGE_SIZE), lambda s, pt: (s, 0, 0)),
                 pl.BlockSpec((1, c, PAGE_SIZE), lambda s, pt: (s, 0, 0)),
                 pl.BlockSpec((rows, PAGE_SIZE), lambda s, pt: (0, 0)),
                 pl.BlockSpec((rows, PAGE_SIZE), lambda s, pt: (0, 0))]
    args = [q_rows] + [kt_src] * n_chunks + [vt_src] * n_chunks + [kt_new, vt_new, row_slope, row_pos]
    if mode == "sel":
        nb = mask.shape[1]
        in_specs += [pl.BlockSpec((1, nb, rows), lambda s, pt: (s, 0, 0)),
                     pl.BlockSpec(block_onehot.shape, lambda s, pt: (0, 0))]
        args += [mask, block_onehot]
    kern = functools.partial(_sample_attn_kernel, mode=mode, n_chunks=n_chunks, kpos0=kpos0, n_new=n_new)
    out_specs = [pl.BlockSpec((1, rows, c), lambda s, pt: (s, 0, 0))]
    out_shape = [jax.ShapeDtypeStruct((n_seq, rows, c), F32)]
    if mode == "win":
        win_len = n_chunks * PAGE_SIZE
        out_specs += [pl.BlockSpec((1, c, win_len), lambda s, pt: (s, 0, 0))] * 2
        out_shape += [jax.ShapeDtypeStruct((n_seq, c, win_len), F32)] * 2
    out = pl.pallas_call(
        kern,
        grid_spec=pltpu.PrefetchScalarGridSpec(
            num_scalar_prefetch=1,
            grid=(n_seq,),
            in_specs=in_specs,
            out_specs=out_specs,
            scratch_shapes=[pltpu.VMEM((rows, (n_chunks + 1) * PAGE_SIZE), F32)]),
        out_shape=out_shape,
        compiler_params=_cparams("parallel"),
        name="sample_attn_" + mode,
    )(page_table, *args)
    return out if mode == "win" else out[0]


def _compress_native_kernel(pt_ref, *refs, n_pages, n_groups):
    n_in = SEQ_PER_STEP * n_pages
    page_refs = refs[:n_in]
    pe_ref, w1_ref, w2_ref, o_ref, stage_ref = refs[n_in:]
    del pt_ref
    steps_per_batch = SEQ_PER_BATCH // SEQ_PER_STEP
    slot = pl.program_id(0) % steps_per_batch
    page_rows = n_groups * HEAD_DIM
    for ip in range(n_in):
        start = pl.multiple_of((slot * n_in + ip) * page_rows, page_rows)
        stage_ref[pl.ds(start, page_rows), :] = page_refs[ip][0]

    @pl.when(slot == steps_per_batch - 1)
    def _():
        n_rows = SEQ_PER_BATCH * n_pages * n_groups
        hid = jnp.zeros((n_rows, PAGE_SIZE), F32)
        for dd in range(HEAD_DIM // 2):
            x0 = stage_ref[pl.ds(2 * dd, n_rows, stride=HEAD_DIM), :] + pe_ref[2 * dd:2 * dd + 1, :]
            x1 = stage_ref[pl.ds(2 * dd + 1, n_rows, stride=HEAD_DIM), :] + pe_ref[2 * dd + 1:2 * dd + 2, :]
            hid = hid + _nn(jnp.concatenate([x0, x1], axis=1).astype(BF16), w1_ref[dd])
        act = (hid * _sigmoid(hid)).astype(BF16)
        o_ref[...] = _nn(act, w2_ref[...])


def compress_native(pool_t, page_table, pe_t, w1_pairs, w2_blocks):
    n_seq, n_pages = page_table.shape
    page_rows = pool_t.shape[1]
    n_groups = page_rows // HEAD_DIM
    out_rows = SEQ_PER_BATCH * n_pages * n_groups
    in_specs = [pl.BlockSpec((1, page_rows, PAGE_SIZE), functools.partial(
        lambda s, pt, i, p: (pt[s * SEQ_PER_STEP + i, p], 0, 0), i=i, p=p))
        for i in range(SEQ_PER_STEP) for p in range(n_pages)]
    in_specs += [pl.BlockSpec(pe_t.shape, lambda s, pt: (0, 0)),
                 pl.BlockSpec(w1_pairs.shape, lambda s, pt: (0, 0, 0)),
                 pl.BlockSpec(w2_blocks.shape, lambda s, pt: (0, 0))]
    steps_per_batch = SEQ_PER_BATCH // SEQ_PER_STEP
    return pl.pallas_call(
        functools.partial(_compress_native_kernel, n_pages=n_pages, n_groups=n_groups),
        grid_spec=pltpu.PrefetchScalarGridSpec(
            num_scalar_prefetch=1,
            grid=(n_seq // SEQ_PER_STEP,),
            in_specs=in_specs,
            out_specs=pl.BlockSpec((out_rows, PAGE_SIZE), lambda s, pt: (s // steps_per_batch, 0)),
            scratch_shapes=[pltpu.VMEM((SEQ_PER_BATCH * n_pages * page_rows, PAGE_SIZE), F32)]),
        out_shape=jax.ShapeDtypeStruct((n_seq * n_pages * n_groups, PAGE_SIZE), F32),
        compiler_params=_cparams("arbitrary"),
        name="compress_native",
    )(page_table, *([pool_t] * (SEQ_PER_STEP * n_pages)), pe_t, w1_pairs, w2_blocks)


NSA_KV_W = NSA_KV_HEADS * HEAD_DIM
NSA_PROJ_W = 3840
NSA_Z_BLK, NSA_KV_BLK0, NSA_GL_BLK = 1, 8, 14


def _split3(x):
    hi = x.astype(BF16).astype(F32)
    mid = (x - hi).astype(BF16).astype(F32)
    return hi, mid, (x - hi - mid).astype(BF16).astype(F32)


def _lane_table(rows, cols):
    body = jnp.stack(cols, axis=-1)
    return jnp.pad(body, ((0, 0), (HEAD_DIM, LANES - HEAD_DIM - len(cols))))


def _flash_tables(slopes, seq):
    assert seq <= 64 * 256, "key positions are split into two bf16-exact pieces"
    hi, mid, lo = _split3(slopes * LOG2E)
    pos = jnp.arange(seq)
    pos_hi, pos_lo = (pos // 64 * 64).astype(F32), (pos % 64).astype(F32)
    return (_lane_table(N_HEADS, [hi, hi, mid, mid, lo, lo]),
            _lane_table(seq, [pos_hi, pos_lo] * (ALIBI_COLS // 2)),
            _lane_table(seq, [jnp.ones((seq,), F32)]))


def _block_onehot(seq, block):
    return (jnp.arange(seq)[:, None] // block == jnp.arange(LANES)[None, :]).astype(BF16)


def _column_tables(slopes, past, n_tok):
    col = jnp.arange(COLS)
    live = col < N_HEADS * n_tok
    head = jnp.minimum(col // n_tok, N_HEADS - 1)
    slope = jnp.where(live, slopes[head], 0.0)
    pos = jnp.where(live, past + col % n_tok, past).astype(jnp.int32)
    grp = jnp.where(live, head // NSA_GROUP, -1).astype(jnp.int32)
    ints = jnp.concatenate([pos[None, :], grp[None, :], jnp.zeros((6, COLS), jnp.int32)], axis=0)
    return jnp.broadcast_to(slope[None, :], (8, COLS)), ints


def _row_tables(col_slope, col_ints):
    return (jnp.broadcast_to(col_slope[0][:, None], (COLS, PAGE_SIZE)),
            jnp.broadcast_to(col_ints[0][:, None], (COLS, PAGE_SIZE)))


def _new_rows_t(t2d, n_seq, n_tok):
    rows_t = t2d.reshape(n_seq, n_tok, -1).transpose(0, 2, 1)
    return jnp.pad(rows_t, ((0, 0), (0, 0), (0, PAGE_SIZE - n_tok)))


def _stored_tiles(cache):
    nd = cache.ndim
    t = jnp.transpose(cache, tuple(range(nd - 3)) + (nd - 2, nd - 1, nd - 3))
    return t.reshape((-1, cache.shape[-2] * cache.shape[-1], cache.shape[-3]))


def _expand_query(q2d, n_seq, n_tok, heads_per_key):
    q4 = (q2d * ATTN_SCALE).reshape(n_seq, n_tok, N_HEADS, HEAD_DIM)
    n_keys = N_HEADS // heads_per_key
    owner = (jnp.arange(N_HEADS)[:, None] // heads_per_key == jnp.arange(n_keys)[None, :])
    qbd = jnp.where(owner[None, None, :, None, :], q4[..., None], 0.0)
    qbd = qbd.transpose(0, 4, 3, 2, 1).reshape(n_seq, n_keys * HEAD_DIM, N_HEADS * n_tok)
    return jnp.pad(qbd, ((0, 0), (0, 0), (0, COLS - N_HEADS * n_tok))).astype(BF16)


def _take_own(o_t, n_seq, n_tok, heads_per_key):
    n_keys = N_HEADS // heads_per_key
    o5 = o_t[:, :N_HEADS * n_tok].reshape(n_seq, N_HEADS, n_tok, n_keys, HEAD_DIM)
    hh = jnp.arange(N_HEADS)
    own = o5[:, hh, :, hh // heads_per_key, :]
    return own.transpose(1, 2, 0, 3).reshape(n_seq * n_tok, N_HEADS * HEAD_DIM)


def _pad_rows(t2d, n_seq, n_tok):
    return jnp.pad(t2d.reshape(n_seq, n_tok, -1), ((0, 0), (0, 8 - n_tok), (0, 0)))


def _nsa_weights(w_in, pe_k, pe_v, w1_k, w2_k, w1_v, w2_v):
    d = w_in.shape[0]
    q, kv, gl, z = (w_in[:, :1024], w_in[:, 1024:1024 + 6 * NSA_KV_W],
                    w_in[:, 1024 + 6 * NSA_KV_W:1024 + 6 * NSA_KV_W + 3 * N_HEADS], w_in[:, -1024:])
    pad = jnp.zeros((d, NSA_PROJ_W - (2048 + 6 * NSA_KV_W + 3 * N_HEADS)), w_in.dtype)
    w_re = jnp.concatenate([q, z, kv, gl, pad], axis=1).astype(BF16)
    eye = jnp.eye(NSA_KV_HEADS, dtype=w1_k.dtype)

    def big(pe, w1, w2):
        hid = w1.shape[1]
        w1r = w1.reshape(CMP_BLOCK, HEAD_DIM, hid)
        w1b = (w1r[:, None, :, None, :] * eye[None, :, None, :, None]).reshape(
            CMP_BLOCK * NSA_KV_W, NSA_KV_HEADS * hid).astype(BF16)
        w2b = (w2[None, :, None, :] * eye[:, None, :, None]).reshape(
            NSA_KV_HEADS * hid, NSA_KV_W).astype(BF16)
        peb = jnp.broadcast_to(pe[:, None, :], (CMP_BLOCK, NSA_KV_HEADS, HEAD_DIM)).reshape(1, -1)
        return peb, w1b, w2b

    bpp = PAGE_SIZE // CMP_BLOCK
    eye_b = jnp.eye(bpp, dtype=w1_k.dtype)

    def stored(pe, w1, w2):
        hid = w1.shape[1]
        w1r = w1.reshape(CMP_BLOCK, HEAD_DIM, hid).transpose(1, 0, 2)
        w1d = (w1r[:, None, :, None, :] * eye_b[None, :, None, :, None]).reshape(
            HEAD_DIM, bpp * CMP_BLOCK, bpp * hid)
        w1p = w1d.reshape(HEAD_DIM // 2, 2 * bpp * CMP_BLOCK, bpp * hid).astype(BF16)
        w2b = (w2[None, :, None, :] * eye_b[:, None, :, None]).reshape(bpp * hid, bpp * HEAD_DIM).astype(BF16)
        return jnp.tile(pe.T, (1, bpp)), w1p, w2b

    return (w_re, big(pe_k, w1_k, w2_k), big(pe_v, w1_v, w2_v), stored(pe_k, w1_k, w2_k),
            stored(pe_v, w1_v, w2_v))


def _nsa_prompt(x, nw, w_re, cmp_k, cmp_v, w_out, slopes, batch, final_w):
    t = x.shape[0]
    n = t // batch
    proj = norm_matmul(x, nw, w_re)
    kv = [proj[:, 2048 + NSA_KV_W * i:2048 + NSA_KV_W * (i + 1)] for i in range(6)]
    ck, cv, sk, sv, wk, wv = kv
    row_w = CMP_BLOCK * NSA_KV_W
    bpp = PAGE_SIZE // CMP_BLOCK
    n_pages = t // PAGE_SIZE
    pt = jnp.arange(n_pages, dtype=jnp.int32).reshape(SEQ_PER_BATCH, n_pages // SEQ_PER_BATCH)
    nb = n // CMP_BLOCK
    comp = lambda rows, wts: compress_paged(rows.reshape(n_pages, bpp, row_w), pt, *wts).reshape(
        batch, nb, NSA_KV_HEADS, HEAD_DIM).transpose(0, 2, 1, 3).astype(BF16)
    o_cmp, mask = nsa_cmp_prompt(proj, comp(ck, cmp_k), comp(cv, cmp_v), slopes, batch)
    slope_tab, pos_tab, ones_tab = _flash_tables(slopes, n)
    blk = NSA_KV_BLK0
    o_sel = flash_prompt(proj, slope_tab, pack_heads(proj, blk + 2, 1, batch, pos_tab, _block_onehot(n, CMP_BLOCK)),
                         pack_heads(proj, blk + 3, 1, batch, ones_tab), batch, mask=mask, tq=256, tk=1024)
    o_win = flash_prompt(proj, slope_tab, pack_heads(proj, blk + 4, 1, batch, pos_tab),
                         pack_heads(proj, blk + 5, 1, batch, ones_tab), batch, window=WINDOW, tq=512, tk=256)
    y = out_proj(x, proj, NSA_Z_BLK, [o_cmp, o_sel, o_win], w_out, gl_blk=NSA_GL_BLK, final_w=final_w)
    st = lambda a: a.reshape(batch, n, NSA_KV_HEADS, HEAD_DIM)
    keep = min(WINDOW, n)
    return y, (st(ck), st(cv), st(sk), st(sv), st(wk)[:, n - keep:], st(wv)[:, n - keep:])


def _nsa_sample(x, nw, w_re, cmp_k, cmp_v, w_out, slopes, n_seq, final_w, page_table, layer_j, caches):
    ck_t, cv_t, sk_t, sv_t, wk_t, wv_t, win_k, win_v, n_pool = caches
    n_tok = x.shape[0] // n_seq
    n_pages = page_table.shape[1]
    past = n_pages * PAGE_SIZE
    win_len = win_k.shape[2]
    bpp = PAGE_SIZE // CMP_BLOCK
    proj = norm_matmul(x, nw, w_re)
    kv = [proj[:, 2048 + NSA_KV_W * i:2048 + NSA_KV_W * (i + 1)] for i in range(6)]
    ck, cv, sk, sv, wk, wv = kv
    pt_layer = page_table + layer_j * n_pool
    comp = lambda pool_t, wts: compress_native(pool_t, pt_layer, *wts).reshape(
        n_seq, n_pages, NSA_KV_HEADS, bpp, HEAD_DIM).transpose(0, 1, 3, 2, 4).reshape(
        n_seq, n_pages * bpp, NSA_KV_W)
    col_slope, col_ints = _column_tables(slopes, past, n_tok)
    qbd = _expand_query(proj[:, :1024], n_seq, n_tok, NSA_GROUP)
    o_cmp, mask = nsa_sample_cmp(qbd, comp(ck_t, cmp_k), comp(cv_t, cmp_v), col_slope, col_ints)
    q_rows = qbd.transpose(0, 2, 1)
    row_slope, row_pos = _row_tables(col_slope, col_ints)
    new_t = lambda a: _new_rows_t(a, n_seq, n_tok)
    block_onehot = (jnp.arange(past // CMP_BLOCK)[:, None] == jnp.arange(past)[None, :] // CMP_BLOCK).astype(BF16)
    o_sel = sample_attn("sel", q_rows, sk_t, sv_t, page_table, new_t(sk), new_t(sv), row_slope, row_pos, 0, n_tok,
                        page0=layer_j * n_pool, mask=mask, block_onehot=block_onehot)
    assert win_len == WINDOW, "the updated window keeps exactly the buffer's length"
    o_win, kw_t, vw_t = sample_attn("win", q_rows, wk_t, wv_t, page_table, new_t(wk), new_t(wv), row_slope, row_pos,
                                    past - win_len, n_tok, page0=layer_j * n_seq, win_chunks=win_len // PAGE_SIZE)
    own = lambda o: _take_own(o, n_seq, n_tok, NSA_GROUP)
    y = out_proj(x, proj, NSA_Z_BLK, [own(o_cmp), own(o_sel), own(o_win)], w_out, gl_blk=NSA_GL_BLK,
                 final_w=final_w)
    st = lambda a: a.reshape(n_seq, n_tok, NSA_KV_HEADS, HEAD_DIM)
    tokens_major = lambda a_t: a_t.reshape(n_seq, NSA_KV_HEADS, HEAD_DIM, win_len).transpose(0, 3, 1, 2)
    return y, (st(ck), st(cv), st(sk), st(sv), tokens_major(kw_t), tokens_major(vw_t))


def _nsa_cache_views(cmp_k, cmp_v, sel_k, sel_v, win_k, win_v):
    n_pool = cmp_k.shape[1]
    return (_stored_tiles(cmp_k), _stored_tiles(cmp_v), _stored_tiles(sel_k), _stored_tiles(sel_v),
            _stored_tiles(win_k), _stored_tiles(win_v), win_k, win_v, n_pool)


def _moba_prompt(x, nw, w_in, w_out, slopes, batch, final_w):
    t = x.shape[0]
    n = t // batch
    proj = norm_matmul(x, nw, w_in)
    k, v = proj[:, 1024:2048], proj[:, 2048:3072]
    nf = n // MOBA_BLOCK
    kmean = block_mean(proj, 1, MOBA_BLOCK).reshape(batch, nf, N_HEADS, HEAD_DIM).transpose(0, 2, 1, 3).astype(BF16)
    mask = moba_select(proj, kmean, batch)
    slope_tab, pos_tab, ones_tab = _flash_tables(slopes, n)
    o = flash_prompt(proj, slope_tab, pack_heads(proj, 4, 4, batch, pos_tab, _block_onehot(n, MOBA_BLOCK)),
                     pack_heads(proj, 8, 4, batch, ones_tab), batch, mask=mask, tq=1024, tk=1024)
    y = out_proj(x, proj, 3, [o], w_out, final_w=final_w)
    st = lambda a: a.reshape(batch, n, N_HEADS, HEAD_DIM)
    return y, (st(k), st(v))


def _moba_sample(x, nw, w_in, w_out, slopes, n_seq, final_w, page_table, k_pool, v_pool):
    n_tok = x.shape[0] // n_seq
    past = page_table.shape[1] * PAGE_SIZE
    assert past % MOBA_BLOCK == 0 and n_tok <= MOBA_BLOCK
    proj = norm_matmul(x, nw, w_in)
    k, v = proj[:, 1024:2048], proj[:, 2048:3072]
    q_rows = _expand_query(proj[:, :1024], n_seq, n_tok, 1).transpose(0, 2, 1)
    row_slope, row_pos = _row_tables(*_column_tables(slopes, past, n_tok))
    o = sample_attn("moba", q_rows, _stored_tiles(k_pool), _stored_tiles(v_pool), page_table,
                    _new_rows_t(k, n_seq, n_tok), _new_rows_t(v, n_seq, n_tok), row_slope, row_pos, 0, n_tok)
    y = out_proj(x, proj, 3, [_take_own(o, n_seq, n_tok, 1)], w_out, final_w=final_w)
    st = lambda a: a.reshape(n_seq, n_tok, N_HEADS, HEAD_DIM)
    return y, (st(k), st(v))


def _ret_prompt_layer(x, nw, w_in, gn_w, gn_b, w_out, batch, final_w):
    proj = norm_matmul(x, nw, w_in)
    on, state = ret_prompt(proj, batch, gn_w, gn_b)
    return out_proj(x, proj, 3, [on], w_out, final_w=final_w), state


def _ret_sample_layer(x, nw, w_in, gn_w, gn_b, w_out, n_seq, final_w, state):
    n_tok = x.shape[0] // n_seq
    proj = norm_matmul(x, nw, w_in)
    on, new_state = ret_sample(_pad_rows(proj, n_seq, n_tok), state, gn_w, gn_b, n_tok)
    on = on[:, :n_tok].reshape(n_seq * n_tok, -1)
    return out_proj(x, proj, 3, [on], w_out, final_w=final_w), new_state


def kernel(x_prompt, x_sample, cache_nsa_cmp_k, cache_nsa_cmp_v, cache_nsa_sel_k, cache_nsa_sel_v,
           cache_nsa_win_k, cache_nsa_win_v, cache_moba_k, cache_moba_v, state_ret, page_table,
           norm_w, final_norm_w, nsa_w_in, nsa_pe_k, nsa_pe_v, nsa_w1_k, nsa_w2_k, nsa_w1_v, nsa_w2_v,
           nsa_w_out, moba_w_in, moba_w_out, ret_w_in, ret_gn_w, ret_gn_b, ret_w_out):
    batch, seq, d = x_prompt.shape
    n_seq, n_tok, _ = x_sample.shape
    depth = norm_w.shape[0]
    slopes = jnp.exp2(-8.0 * (jnp.arange(N_HEADS, dtype=F32) + 1.0) / N_HEADS)
    xp = x_prompt.reshape(batch * seq, d)
    xs = x_sample.reshape(n_seq * n_tok, d)
    nsa_p, nsa_s, moba_p, moba_s, ret_p, ret_s = [], [], [], [], [], []
    nsa_caches = _nsa_cache_views(cache_nsa_cmp_k, cache_nsa_cmp_v, cache_nsa_sel_k, cache_nsa_sel_v,
                                  cache_nsa_win_k, cache_nsa_win_v)
    for layer in range(depth):
        j = layer // N_MIXERS
        fw = final_norm_w if layer == depth - 1 else None
        nw = norm_w[layer]
        if layer % N_MIXERS == 0:
            w_re, cmp_k, cmp_v, cmp_k_st, cmp_v_st = _nsa_weights(
                nsa_w_in[j], nsa_pe_k[j], nsa_pe_v[j], nsa_w1_k[j], nsa_w2_k[j], nsa_w1_v[j], nsa_w2_v[j])
            w_out = nsa_w_out[j].astype(BF16)
            xp, stp = _nsa_prompt(xp, nw, w_re, cmp_k, cmp_v, w_out, slopes, batch, fw)
            xs, sts = _nsa_sample(xs, nw, w_re, cmp_k_st, cmp_v_st, w_out, slopes, n_seq, fw, page_table, j,
                                  nsa_caches)
            nsa_p.append(stp)
            nsa_s.append(sts)
        elif layer % N_MIXERS == 1:
            w_in, w_out = moba_w_in[j].astype(BF16), moba_w_out[j].astype(BF16)
            xp, stp = _moba_prompt(xp, nw, w_in, w_out, slopes, batch, fw)
            xs, sts = _moba_sample(xs, nw, w_in, w_out, slopes, n_seq, fw, page_table, cache_moba_k[j],
                                   cache_moba_v[j])
            moba_p.append(stp)
            moba_s.append(sts)
        else:
            w_in, w_out = ret_w_in[j].astype(BF16), ret_w_out[j].astype(BF16)
            xp, stp = _ret_prompt_layer(xp, nw, w_in, ret_gn_w[j], ret_gn_b[j], w_out, batch, fw)
            xs, sts = _ret_sample_layer(xs, nw, w_in, ret_gn_w[j], ret_gn_b[j], w_out, n_seq, fw, state_ret[j])
            ret_p.append(stp)
            ret_s.append(sts)
    st = lambda items, i: jnp.stack([s[i] for s in items])
    return (xp.reshape(batch, seq, d), xs.reshape(n_seq, n_tok, d),
            st(nsa_p, 0), st(nsa_p, 1), st(nsa_p, 2), st(nsa_p, 3), st(nsa_p, 4), st(nsa_p, 5),
            st(moba_p, 0), st(moba_p, 1), jnp.stack(ret_p),
            st(nsa_s, 0), st(nsa_s, 1), st(nsa_s, 2), st(nsa_s, 3), st(nsa_s, 4), st(nsa_s, 5),
            st(moba_s, 0), st(moba_s, 1), jnp.stack(ret_s))
```
